```python
import jax, jax.numpy as jnp
from jax import lax
import numpy as np

D_MODEL = 1024
BATCH = 32
SEQ = 256
DEPTH = 2
DEC_BATCH = 2
DEC_SEQ = 4096
PAST_LEN = 256

GRID_W = 64
N_MOD = 6
EPS = 1e-6
N_HEADS = 8
N_KV_HEADS = 2
HEAD_DIM = 64
ATTN_W = N_HEADS * HEAD_DIM
KV_W = N_KV_HEADS * HEAD_DIM
Q_BLOCK = 128
ROPE_THETA = 10000.0
HY_CH = 256
HY_GROUPS = 4
HY_SHORT = 3
HY_BANDS = 16
HY_EMB = 1 + 2 * HY_BANDS
HY_FFN = 64
HY_FILTER_STD = 0.01
HY_DECAY_MIN = 3.07
HY_DECAY_MAX = 15.35
SG_W = 256
SG_GROUPS = 4
SG_GROUP_W = SG_W // SG_GROUPS
CHUNK = 128
MIX_W = ATTN_W + HY_CH + SG_W
Q_END = ATTN_W
K_END = Q_END + KV_W
V_END = K_END + KV_W
HY_END = V_END + 3 * HY_CH
IN_W = HY_END + 2 * SG_W
PEER_HEADS = 8
N_KEYS = 128
N_EXPERTS = N_KEYS * N_KEYS
PEER_KEY_DIM = 128
PEER_HALF = PEER_KEY_DIM // 2
PEER_TOPK = 16
TOKEN_BLOCK = 128

kernel_name = 'hyena_gmlp_gqa_peer_prefix_dit_step'

F32 = jnp.float32


def rmsnorm(x, g):
    xf = x.astype(F32)
    y = xf * lax.rsqrt(jnp.mean(xf * xf, axis=-1, keepdims=True) + EPS)
    return (y * g.astype(F32)).astype(x.dtype)


def rope_2d_tables(L):
    rows = L // GRID_W
    row = jnp.broadcast_to(jnp.arange(rows, dtype=F32)[:, None], (rows, GRID_W)).reshape(L)
    col = jnp.broadcast_to(jnp.arange(GRID_W, dtype=F32)[None, :], (rows, GRID_W)).reshape(L)
    n_axis = HEAD_DIM // 4
    inv = ROPE_THETA ** (-jnp.arange(n_axis, dtype=F32) / n_axis)
    ang = jnp.concatenate([row[:, None] * inv, col[:, None] * inv], axis=-1)
    return jnp.cos(ang), jnp.sin(ang)


def apply_rope(x, cos, sin):
    B, L, H, _ = x.shape
    xp = x.astype(F32).reshape(B, L, H, HEAD_DIM // 2, 2)
    a, b = xp[..., 0], xp[..., 1]
    c = cos[None, :, None, :]
    s = sin[None, :, None, :]
    out = jnp.stack([a * c - b * s, a * s + b * c], axis=-1).reshape(B, L, H, HEAD_DIM)
    return out.astype(x.dtype)


def attention(q, k, v):
    B, Lq, H, Dh = q.shape
    G = N_HEADS // N_KV_HEADS
    nb = Lq // Q_BLOCK
    qb = q.reshape(B, nb, Q_BLOCK, N_KV_HEADS, G, Dh).transpose(1, 0, 2, 3, 4, 5)
    scale = HEAD_DIM ** -0.5

    def block(qblk):
        s = jnp.einsum('bqkgd,bskd->bkgqs', qblk, k).astype(F32) * scale
        p = jax.nn.softmax(s, axis=-1).astype(v.dtype)
        return jnp.einsum('bkgqs,bskd->bqkgd', p, v)

    o = lax.map(block, qb)
    return o.transpose(1, 0, 2, 3, 4, 5).reshape(B, Lq, H * Dh)


def short_conv(z, w, b):
    zp = jnp.pad(z, ((0, 0), (1, 1), (0, 0)))
    return zp[:, :-2] * w[0] + zp[:, 1:-1] * w[1] + zp[:, 2:] * w[2] + b


def hyena_filter_fft(L, fw1, fb1, fr1, fw2, fb2, fr2, fw3, decay):
    t = jnp.arange(L, dtype=F32)
    tn = t / (L - 1)
    w = 2.0 * jnp.pi * t / L
    f = jnp.linspace(1e-4, HY_BANDS - 1, HY_BANDS, dtype=F32)
    zf = w[:, None] * f[None, :]
    feats = jnp.concatenate([tn[:, None], jnp.cos(zf), -jnp.sin(zf)], axis=-1)
    h = jnp.sin(fr1.astype(F32) * (feats @ fw1.astype(F32) + fb1.astype(F32)))
    h = jnp.sin(fr2.astype(F32) * (h @ fw2.astype(F32) + fb2.astype(F32)))
    h = (h @ fw3.astype(F32)).reshape(L, 2, HY_CH)
    h = h * jnp.exp(-tn[:, None, None] * jnp.abs(decay.astype(F32)))
    fwd, bwd = h[:, 0], h[:, 1]
    full = jnp.concatenate([fwd, jnp.zeros((1, HY_CH), F32), bwd[:0:-1]], axis=0)
    return jnp.fft.rfft(full, axis=0)


def hyena(z, P):
    B, L, _ = z.shape
    z = short_conv(z, P['hy_conv_w'], P['hy_conv_b'])
    x0, x1, v = jnp.split(z, 3, axis=-1)
    u = (x1 * v).astype(F32)
    Hf = hyena_filter_fft(L, P['hy_fw1'], P['hy_fb1'], P['hy_freq1'], P['hy_fw2'],
                          P['hy_fb2'], P['hy_freq2'], P['hy_fw3'], P['hy_decay'])
    U = jnp.fft.rfft(u, n=2 * L, axis=1)
    conv = jnp.fft.irfft(U * Hf[None], n=2 * L, axis=1)[:, :L]
    y = x0.astype(F32) * (conv + u * P['hy_skip'].astype(F32))
    return y.astype(z.dtype)


def chunk_gmlp(z, g, ws, bs):
    B, L, _ = z.shape
    u, v = jnp.split(jax.nn.gelu(z), 2, axis=-1)
    v = rmsnorm(v, g).reshape(B, L // CHUNK, CHUNK, SG_GROUPS, SG_GROUP_W)
    v = jnp.einsum('hpq,bnqhc->bnphc', ws, v) + bs.T[None, None, :, :, None]
    return u * v.reshape(B, L, SG_W)


def peer(h, wq, keys, u_tab, v_tab):
    B, L, D = h.shape
    T = B * L
    x = h.reshape(T, D)
    q = (x @ wq).reshape(T, PEER_HEADS, 2, PEER_HALF)
    s = jnp.einsum('thpd,hpnd->thpn', q, keys).astype(F32)
    s_top, i_top = lax.top_k(s, PEER_TOPK)
    cand = (s_top[:, :, 0, :, None] + s_top[:, :, 1, None, :]).reshape(T, PEER_HEADS, PEER_TOPK * PEER_TOPK)
    cidx = (i_top[:, :, 0, :, None] * N_KEYS + i_top[:, :, 1, None, :]).reshape(T, PEER_HEADS, PEER_TOPK * PEER_TOPK)
    sc, pos = lax.top_k(cand, PEER_TOPK)
    idx = jnp.take_along_axis(cidx, pos, axis=-1)
    gate = jax.nn.softmax(sc, axis=-1)
    nb = T // TOKEN_BLOCK
    E = PEER_HEADS * PEER_TOPK
    xb = x.reshape(nb, TOKEN_BLOCK, D)
    ib = idx.reshape(nb, TOKEN_BLOCK, E)
    gb = gate.reshape(nb, TOKEN_BLOCK, E).astype(x.dtype)

    def block(args):
        xt, it, gt = args
        ue = jnp.take(u_tab, it, axis=0)
        ve = jnp.take(v_tab, it, axis=0)
        a = jax.nn.gelu(jnp.einsum('td,ted->te', xt, ue))
        return jnp.einsum('te,ted->td', gt * a, ve)

    out = lax.map(block, (xb, ib, gb))
    return out.reshape(B, L, D)


def layer(x, mod, P, rope, k_ext, v_ext):
    B, L, _ = x.shape
    sh_m, sc_m, g_m, sh_f, sc_f, g_f = jnp.split(mod, N_MOD, axis=-1)
    h = rmsnorm(x, P['norm_mix_g']) * (1.0 + sc_m) + sh_m
    proj = h @ P['w_in']
    q = rmsnorm(proj[..., :Q_END].reshape(B, L, N_HEADS, HEAD_DIM), P['q_norm_g'])
    k = rmsnorm(proj[..., Q_END:K_END].reshape(B, L, N_KV_HEADS, HEAD_DIM), P['k_norm_g'])
    v = proj[..., K_END:V_END].reshape(B, L, N_KV_HEADS, HEAD_DIM)
    if rope is None:
        attn = attention(q, k, v)
    else:
        cos, sin = rope
        q_r = apply_rope(q, cos, sin)
        k_r = apply_rope(k, cos, sin)
        k_all = jnp.concatenate([k_ext.astype(k_r.dtype), k_r], axis=1)
        v_all = jnp.concatenate([v_ext.astype(v.dtype), v], axis=1)
        attn = attention(q_r, k_all, v_all)
    hy = hyena(proj[..., V_END:HY_END], P)
    sg = chunk_gmlp(proj[..., HY_END:], P['sg_norm_g'], P['sg_w'], P['sg_b'])
    mixed = jnp.concatenate([attn, hy.astype(attn.dtype), sg.astype(attn.dtype)], axis=-1) @ P['w_out']
    x = x + g_m * mixed
    h = rmsnorm(x, P['norm_ffn_g']) * (1.0 + sc_f) + sh_f
    x = x + g_f * peer(h, P['peer_wq'], P['peer_keys'], P['peer_u'], P['peer_v'])
    return x, k, v


def setup_inputs(seed: int = 0) -> dict:
    key = jax.random.key(seed)
    ks = iter(jax.random.split(key, 40))
    D = D_MODEL

    def nrm(shape, std):
        return std * jax.random.normal(next(ks), shape, jnp.float32)

    def gain(shape):
        return 1.0 + nrm(shape, 0.02)

    return {
        'x_prompt': nrm((BATCH, SEQ, D), 1.0),
        'x_sample': nrm((DEC_BATCH, DEC_SEQ, D), 1.0),
        'cache_k': nrm((DEC_BATCH, DEPTH, PAST_LEN, N_KV_HEADS, HEAD_DIM), 1.0),
        'cache_v': nrm((DEC_BATCH, DEPTH, PAST_LEN, N_KV_HEADS, HEAD_DIM), 1.0),
        'c': nrm((DEC_BATCH, D), 1.0),
        'c_ctx': nrm((D,), 1.0),
        'norm_mix_g': gain((DEPTH, D)),
        'norm_ffn_g': gain((DEPTH, D)),
        'ada_w': nrm((DEPTH, D, N_MOD * D), D ** -0.5),
        'ada_b': nrm((DEPTH, N_MOD * D), 0.01),
        'w_in': nrm((DEPTH, D, IN_W), D ** -0.5),
        'w_out': nrm((DEPTH, MIX_W, D), MIX_W ** -0.5),
        'q_norm_g': gain((DEPTH, HEAD_DIM)),
        'k_norm_g': gain((DEPTH, HEAD_DIM)),
        'hy_conv_w': nrm((DEPTH, HY_SHORT, 3 * HY_CH), HY_SHORT ** -0.5),
        'hy_conv_b': nrm((DEPTH, 3 * HY_CH), 0.01),
        'hy_fw1': nrm((DEPTH, HY_EMB, HY_FFN), HY_EMB ** -0.5),
        'hy_fb1': nrm((DEPTH, HY_FFN), 0.1),
        'hy_freq1': 1.0 + nrm((DEPTH, HY_FFN), 0.1),
        'hy_fw2': nrm((DEPTH, HY_FFN, HY_FFN), HY_FFN ** -0.5),
        'hy_fb2': nrm((DEPTH, HY_FFN), 0.1),
        'hy_freq2': 1.0 + nrm((DEPTH, HY_FFN), 0.1),
        'hy_fw3': nrm((DEPTH, HY_FFN, 2 * HY_CH), HY_FILTER_STD),
        'hy_decay': jax.random.uniform(next(ks), (DEPTH, 2, HY_CH), jnp.float32, HY_DECAY_MIN, HY_DECAY_MAX),
        'hy_skip': nrm((DEPTH, HY_CH), 1.0),
        'sg_norm_g': gain((DEPTH, SG_W)),
        'sg_w': nrm((DEPTH, SG_GROUPS, CHUNK, CHUNK), CHUNK ** -0.5),
        'sg_b': 1.0 + nrm((DEPTH, SG_GROUPS, CHUNK), 0.02),
        'peer_wq': nrm((DEPTH, D, PEER_HEADS * PEER_KEY_DIM), D ** -0.5),
        'peer_keys': nrm((DEPTH, PEER_HEADS, 2, N_KEYS, PEER_HALF), PEER_HALF ** -0.5),
        'peer_u': nrm((DEPTH, N_EXPERTS, D), D ** -0.5),
        'peer_v': nrm((DEPTH, N_EXPERTS, D), (PEER_HEADS * PEER_TOPK) ** -0.5),
    }


def reference(x_prompt, x_sample, cache_k, cache_v, c, c_ctx, norm_mix_g, norm_ffn_g, ada_w, ada_b,
              w_in, w_out, q_norm_g, k_norm_g, hy_conv_w, hy_conv_b, hy_fw1, hy_fb1, hy_freq1,
              hy_fw2, hy_fb2, hy_freq2, hy_fw3, hy_decay, hy_skip, sg_norm_g, sg_w, sg_b,
              peer_wq, peer_keys, peer_u, peer_v):
    rope = rope_2d_tables(x_sample.shape[1])
    y_p = x_prompt
    y_s = x_sample
    ks_ctx = []
    vs_ctx = []
    for l in range(DEPTH):
        P = {
            'norm_mix_g': norm_mix_g[l], 'norm_ffn_g': norm_ffn_g[l],
            'w_in': w_in[l], 'w_out': w_out[l],
            'q_norm_g': q_norm_g[l], 'k_norm_g': k_norm_g[l],
            'hy_conv_w': hy_conv_w[l], 'hy_conv_b': hy_conv_b[l],
            'hy_fw1': hy_fw1[l], 'hy_fb1': hy_fb1[l], 'hy_freq1': hy_freq1[l],
            'hy_fw2': hy_fw2[l], 'hy_fb2': hy_fb2[l], 'hy_freq2': hy_freq2[l],
            'hy_fw3': hy_fw3[l], 'hy_decay': hy_decay[l], 'hy_skip': hy_skip[l],
            'sg_norm_g': sg_norm_g[l], 'sg_w': sg_w[l], 'sg_b': sg_b[l],
            'peer_wq': peer_wq[l], 'peer_keys': peer_keys[l],
            'peer_u': peer_u[l], 'peer_v': peer_v[l],
        }
        mod_ctx = (jax.nn.silu(c_ctx)[None, :] @ ada_w[l] + ada_b[l])[:, None, :]
        mod_lat = (jax.nn.silu(c) @ ada_w[l] + ada_b[l])[:, None, :]
        y_p, k_l, v_l = layer(y_p, mod_ctx, P, None, None, None)
        ks_ctx.append(k_l)
        vs_ctx.append(v_l)
        y_s, _, _ = layer(y_s, mod_lat, P, rope, cache_k[:, l], cache_v[:, l])
    new_cache_k = jnp.stack(ks_ctx, axis=1)
    new_cache_v = jnp.stack(vs_ctx, axis=1)
    return (y_p, y_s, new_cache_k, new_cache_v)
```

```python
import functools

import jax
import jax.numpy as jnp
from jax import lax
from jax.experimental import pallas as pl
from jax.experimental.pallas import tpu as pltpu

D_MODEL = 1024
DEPTH = 2
GRID_W = 64
N_MOD = 6
EPS = 1e-6
N_HEADS = 8
N_KV_HEADS = 2
HEAD_DIM = 64
ATTN_W = N_HEADS * HEAD_DIM
KV_W = N_KV_HEADS * HEAD_DIM
Q_BLOCK = 128
ROPE_THETA = 10000.0
HY_CH = 256
HY_BANDS = 16
SG_W = 256
SG_GROUPS = 4
SG_GROUP_W = SG_W // SG_GROUPS
CHUNK = 128
Q_END = ATTN_W
K_END = Q_END + KV_W
V_END = K_END + KV_W
HY_END = V_END + 3 * HY_CH
PEER_HEADS = 8
N_KEYS = 128
PEER_KEY_DIM = 128
PEER_HALF = PEER_KEY_DIM // 2
PEER_TOPK = 16
TOKEN_BLOCK = 128

F32 = jnp.float32
BF16 = jnp.bfloat16


def _mm_kernel(a_ref, b_ref, o_ref):
    o_ref[...] = jnp.dot(a_ref[...].astype(BF16), b_ref[...].astype(BF16),
                         preferred_element_type=F32)


def _mm(a, b, tm=512, tn=512):
    M, K = a.shape
    _, N = b.shape
    tm = min(tm, M)
    tn = min(tn, N)
    return pl.pallas_call(
        _mm_kernel,
        grid=(M // tm, N // tn),
        in_specs=[pl.BlockSpec((tm, K), lambda i, j: (i, 0)),
                  pl.BlockSpec((K, tn), lambda i, j: (0, j))],
        out_specs=pl.BlockSpec((tm, tn), lambda i, j: (i, j)),
        out_shape=jax.ShapeDtypeStruct((M, N), F32),
    )(a, b)


def _rmsnorm(x, g):
    xf = x.astype(F32)
    y = xf * lax.rsqrt(jnp.mean(xf * xf, axis=-1, keepdims=True) + EPS)
    return (y * g.astype(F32)).astype(x.dtype)


def _rope_tables(L):
    rows = L // GRID_W
    row = jnp.broadcast_to(jnp.arange(rows, dtype=F32)[:, None], (rows, GRID_W)).reshape(L)
    col = jnp.broadcast_to(jnp.arange(GRID_W, dtype=F32)[None, :], (rows, GRID_W)).reshape(L)
    n_axis = HEAD_DIM // 4
    inv = ROPE_THETA ** (-jnp.arange(n_axis, dtype=F32) / n_axis)
    ang = jnp.concatenate([row[:, None] * inv, col[:, None] * inv], axis=-1)
    return jnp.cos(ang), jnp.sin(ang)


def _apply_rope(x, cos, sin):
    B, L, H, _ = x.shape
    xp = x.astype(F32).reshape(B, L, H, HEAD_DIM // 2, 2)
    a, b = xp[..., 0], xp[..., 1]
    c = cos[None, :, None, :]
    s = sin[None, :, None, :]
    out = jnp.stack([a * c - b * s, a * s + b * c], axis=-1).reshape(B, L, H, HEAD_DIM)
    return out.astype(x.dtype)


def _attention(q, k, v):
    B, Lq, H, Dh = q.shape
    G = N_HEADS // N_KV_HEADS
    nb = Lq // Q_BLOCK
    qb = q.reshape(B, nb, Q_BLOCK, N_KV_HEADS, G, Dh).transpose(1, 0, 2, 3, 4, 5)
    scale = HEAD_DIM ** -0.5

    def block(qblk):
        s = jnp.einsum('bqkgd,bskd->bkgqs', qblk, k).astype(F32) * scale
        p = jax.nn.softmax(s, axis=-1).astype(v.dtype)
        return jnp.einsum('bkgqs,bskd->bqkgd', p, v)

    o = lax.map(block, qb)
    return o.transpose(1, 0, 2, 3, 4, 5).reshape(B, Lq, H * Dh)


def _short_conv(z, w, b):
    zp = jnp.pad(z, ((0, 0), (1, 1), (0, 0)))
    return zp[:, :-2] * w[0] + zp[:, 1:-1] * w[1] + zp[:, 2:] * w[2] + b


def _hyena_filter_fft(L, fw1, fb1, fr1, fw2, fb2, fr2, fw3, decay):
    t = jnp.arange(L, dtype=F32)
    tn = t / (L - 1)
    w = 2.0 * jnp.pi * t / L
    f = jnp.linspace(1e-4, HY_BANDS - 1, HY_BANDS, dtype=F32)
    zf = w[:, None] * f[None, :]
    feats = jnp.concatenate([tn[:, None], jnp.cos(zf), -jnp.sin(zf)], axis=-1)
    h = jnp.sin(fr1 * (feats @ fw1 + fb1))
    h = jnp.sin(fr2 * (h @ fw2 + fb2))
    h = (h @ fw3).reshape(L, 2, HY_CH)
    h = h * jnp.exp(-tn[:, None, None] * jnp.abs(decay))
    fwd, bwd = h[:, 0], h[:, 1]
    full = jnp.concatenate([fwd, jnp.zeros((1, HY_CH), F32), bwd[:0:-1]], axis=0)
    return jnp.fft.rfft(full, axis=0)


def _hyena(z, P):
    B, L, _ = z.shape
    z = _short_conv(z, P['hy_conv_w'], P['hy_conv_b'])
    x0, x1, v = jnp.split(z, 3, axis=-1)
    u = (x1 * v).astype(F32)
    Hf = _hyena_filter_fft(L, P['hy_fw1'], P['hy_fb1'], P['hy_freq1'], P['hy_fw2'],
                           P['hy_fb2'], P['hy_freq2'], P['hy_fw3'], P['hy_decay'])
    U = jnp.fft.rfft(u, n=2 * L, axis=1)
    conv = jnp.fft.irfft(U * Hf[None], n=2 * L, axis=1)[:, :L]
    return x0 * (conv + u * P['hy_skip'])


def _chunk_gmlp(z, g, ws, bs):
    B, L, _ = z.shape
    u, v = jnp.split(jax.nn.gelu(z), 2, axis=-1)
    v = _rmsnorm(v, g).reshape(B, L // CHUNK, CHUNK, SG_GROUPS, SG_GROUP_W)
    v = jnp.einsum('hpq,bnqhc->bnphc', ws, v) + bs.T[None, None, :, :, None]
    return u * v.reshape(B, L, SG_W)


def _peer(h, wq, keys, u_tab, v_tab):
    B, L, D = h.shape
    T = B * L
    x = h.reshape(T, D)
    q = _mm(x, wq).reshape(T, PEER_HEADS, 2, PEER_HALF)
    s = jnp.einsum('thpd,hpnd->thpn', q, keys).astype(F32)
    s_top, i_top = lax.top_k(s, PEER_TOPK)
    cand = (s_top[:, :, 0, :, None] + s_top[:, :, 1, None, :]).reshape(T, PEER_HEADS, PEER_TOPK * PEER_TOPK)
    cidx = (i_top[:, :, 0, :, None] * N_KEYS + i_top[:, :, 1, None, :]).reshape(T, PEER_HEADS, PEER_TOPK * PEER_TOPK)
    sc, pos = lax.top_k(cand, PEER_TOPK)
    idx = jnp.take_along_axis(cidx, pos, axis=-1)
    gate = jax.nn.softmax(sc, axis=-1)
    nb = T // TOKEN_BLOCK
    E = PEER_HEADS * PEER_TOPK
    xb = x.reshape(nb, TOKEN_BLOCK, D)
    ib = idx.reshape(nb, TOKEN_BLOCK, E)
    gb = gate.reshape(nb, TOKEN_BLOCK, E)

    def block(args):
        xt, it, gt = args
        ue = jnp.take(u_tab, it, axis=0)
        ve = jnp.take(v_tab, it, axis=0)
        a = jax.nn.gelu(jnp.einsum('td,ted->te', xt, ue))
        return jnp.einsum('te,ted->td', gt * a, ve)

    out = lax.map(block, (xb, ib, gb))
    return out.reshape(B, L, D)


def _layer(x, mod, P, rope, k_ext, v_ext):
    B, L, D = x.shape
    sh_m, sc_m, g_m, sh_f, sc_f, g_f = jnp.split(mod, N_MOD, axis=-1)
    h = _rmsnorm(x, P['norm_mix_g']) * (1.0 + sc_m) + sh_m
    proj = _mm(h.reshape(B * L, D), P['w_in']).reshape(B, L, -1)
    q = _rmsnorm(proj[..., :Q_END].reshape(B, L, N_HEADS, HEAD_DIM), P['q_norm_g'])
    k = _rmsnorm(proj[..., Q_END:K_END].reshape(B, L, N_KV_HEADS, HEAD_DIM), P['k_norm_g'])
    v = proj[..., K_END:V_END].reshape(B, L, N_KV_HEADS, HEAD_DIM)
    if rope is None:
        attn = _attention(q, k, v)
    else:
        cos, sin = rope
        q_r = _apply_rope(q, cos, sin)
        k_r = _apply_rope(k, cos, sin)
        k_all = jnp.concatenate([k_ext, k_r], axis=1)
        v_all = jnp.concatenate([v_ext, v], axis=1)
        attn = _attention(q_r, k_all, v_all)
    hy = _hyena(proj[..., V_END:HY_END], P)
    sg = _chunk_gmlp(proj[..., HY_END:], P['sg_norm_g'], P['sg_w'], P['sg_b'])
    cat = jnp.concatenate([attn, hy, sg], axis=-1)
    mixed = _mm(cat.reshape(B * L, -1), P['w_out']).reshape(B, L, D)
    x = x + g_m * mixed
    h = _rmsnorm(x, P['norm_ffn_g']) * (1.0 + sc_f) + sh_f
    x = x + g_f * _peer(h, P['peer_wq'], P['peer_keys'], P['peer_u'], P['peer_v'])
    return x, k, v


def kernel(x_prompt, x_sample, cache_k, cache_v, c, c_ctx, norm_mix_g, norm_ffn_g, ada_w, ada_b,
           w_in, w_out, q_norm_g, k_norm_g, hy_conv_w, hy_conv_b, hy_fw1, hy_fb1, hy_freq1,
           hy_fw2, hy_fb2, hy_freq2, hy_fw3, hy_decay, hy_skip, sg_norm_g, sg_w, sg_b,
           peer_wq, peer_keys, peer_u, peer_v):
    rope = _rope_tables(x_sample.shape[1])
    y_p = x_prompt
    y_s = x_sample
    ks_ctx = []
    vs_ctx = []
    for l in range(DEPTH):
        P = {
            'norm_mix_g': norm_mix_g[l], 'norm_ffn_g': norm_ffn_g[l],
            'w_in': w_in[l], 'w_out': w_out[l],
            'q_norm_g': q_norm_g[l], 'k_norm_g': k_norm_g[l],
            'hy_conv_w': hy_conv_w[l], 'hy_conv_b': hy_conv_b[l],
            'hy_fw1': hy_fw1[l], 'hy_fb1': hy_fb1[l], 'hy_freq1': hy_freq1[l],
            'hy_fw2': hy_fw2[l], 'hy_fb2': hy_fb2[l], 'hy_freq2': hy_freq2[l],
            'hy_fw3': hy_fw3[l], 'hy_decay': hy_decay[l], 'hy_skip': hy_skip[l],
            'sg_norm_g': sg_norm_g[l], 'sg_w': sg_w[l], 'sg_b': sg_b[l],
            'peer_wq': peer_wq[l], 'peer_keys': peer_keys[l],
            'peer_u': peer_u[l], 'peer_v': peer_v[l],
        }
        mod_ctx = (jax.nn.silu(c_ctx)[None, :] @ ada_w[l] + ada_b[l])[:, None, :]
        mod_lat = (jax.nn.silu(c) @ ada_w[l] + ada_b[l])[:, None, :]
        y_p, k_l, v_l = _layer(y_p, mod_ctx, P, None, None, None)
        ks_ctx.append(k_l)
        vs_ctx.append(v_l)
        y_s, _, _ = _layer(y_s, mod_lat, P, rope, cache_k[:, l], cache_v[:, l])
    new_cache_k = jnp.stack(ks_ctx, axis=1)
    new_cache_v = jnp.stack(vs_ctx, axis=1)
    return (y_p, y_s, new_cache_k, new_cache_v)
```

```python
import functools

import jax
import jax.numpy as jnp
from jax import lax
from jax.experimental import pallas as pl
from jax.experimental.pallas import tpu as pltpu

D_MODEL = 1024
DEPTH = 2
GRID_W = 64
N_MOD = 6
EPS = 1e-6
N_HEADS = 8
N_KV_HEADS = 2
HEAD_DIM = 64
ATTN_W = N_HEADS * HEAD_DIM
KV_W = N_KV_HEADS * HEAD_DIM
Q_BLOCK = 128
ROPE_THETA = 10000.0
HY_CH = 256
HY_BANDS = 16
SG_W = 256
SG_GROUPS = 4
SG_GROUP_W = SG_W // SG_GROUPS
CHUNK = 128
Q_END = ATTN_W
K_END = Q_END + KV_W
V_END = K_END + KV_W
HY_END = V_END + 3 * HY_CH
PEER_HEADS = 8
N_KEYS = 128
PEER_KEY_DIM = 128
PEER_HALF = PEER_KEY_DIM // 2
PEER_TOPK = 16
TOKEN_BLOCK = 128

F32 = jnp.float32
BF16 = jnp.bfloat16


def _mm_kernel(a_ref, b_ref, o_ref):
    o_ref[...] = jnp.dot(a_ref[...].astype(BF16), b_ref[...].astype(BF16),
                         preferred_element_type=F32)


def _mm(a, b, tm=512, tn=512):
    M, K = a.shape
    _, N = b.shape
    tm = min(tm, M)
    tn = min(tn, N)
    return pl.pallas_call(
        _mm_kernel,
        grid=(M // tm, N // tn),
        in_specs=[pl.BlockSpec((tm, K), lambda i, j: (i, 0)),
                  pl.BlockSpec((K, tn), lambda i, j: (0, j))],
        out_specs=pl.BlockSpec((tm, tn), lambda i, j: (i, j)),
        out_shape=jax.ShapeDtypeStruct((M, N), F32),
    )(a, b)


def _rmsnorm(x, g):
    xf = x.astype(F32)
    y = xf * lax.rsqrt(jnp.mean(xf * xf, axis=-1, keepdims=True) + EPS)
    return (y * g.astype(F32)).astype(x.dtype)


def _rope_tables(L):
    rows = L // GRID_W
    row = jnp.broadcast_to(jnp.arange(rows, dtype=F32)[:, None], (rows, GRID_W)).reshape(L)
    col = jnp.broadcast_to(jnp.arange(GRID_W, dtype=F32)[None, :], (rows, GRID_W)).reshape(L)
    n_axis = HEAD_DIM // 4
    inv = ROPE_THETA ** (-jnp.arange(n_axis, dtype=F32) / n_axis)
    ang = jnp.concatenate([row[:, None] * inv, col[:, None] * inv], axis=-1)
    return jnp.cos(ang), jnp.sin(ang)


def _apply_rope(x, cos, sin):
    B, L, H, _ = x.shape
    xp = x.astype(F32).reshape(B, L, H, HEAD_DIM // 2, 2)
    a, b = xp[..., 0], xp[..., 1]
    c = cos[None, :, None, :]
    s = sin[None, :, None, :]
    out = jnp.stack([a * c - b * s, a * s + b * c], axis=-1).reshape(B, L, H, HEAD_DIM)
    return out.astype(x.dtype)


def _attention(q, k, v):
    B, Lq, H, Dh = q.shape
    G = N_HEADS // N_KV_HEADS
    nb = Lq // Q_BLOCK
    qb = q.reshape(B, nb, Q_BLOCK, N_KV_HEADS, G, Dh).transpose(1, 0, 2, 3, 4, 5)
    scale = HEAD_DIM ** -0.5

    def block(qblk):
        s = jnp.einsum('bqkgd,bskd->bkgqs', qblk, k).astype(F32) * scale
        p = jax.nn.softmax(s, axis=-1).astype(v.dtype)
        return jnp.einsum('bkgqs,bskd->bqkgd', p, v)

    o = lax.map(block, qb)
    return o.transpose(1, 0, 2, 3, 4, 5).reshape(B, Lq, H * Dh)


def _short_conv(z, w, b):
    zp = jnp.pad(z, ((0, 0), (1, 1), (0, 0)))
    return zp[:, :-2] * w[0] + zp[:, 1:-1] * w[1] + zp[:, 2:] * w[2] + b


def _hyena_filter_fft(L, fw1, fb1, fr1, fw2, fb2, fr2, fw3, decay):
    t = jnp.arange(L, dtype=F32)
    tn = t / (L - 1)
    w = 2.0 * jnp.pi * t / L
    f = jnp.linspace(1e-4, HY_BANDS - 1, HY_BANDS, dtype=F32)
    zf = w[:, None] * f[None, :]
    feats = jnp.concatenate([tn[:, None], jnp.cos(zf), -jnp.sin(zf)], axis=-1)
    h = jnp.sin(fr1 * (feats @ fw1 + fb1))
    h = jnp.sin(fr2 * (h @ fw2 + fb2))
    h = (h @ fw3).reshape(L, 2, HY_CH)
    h = h * jnp.exp(-tn[:, None, None] * jnp.abs(decay))
    fwd, bwd = h[:, 0], h[:, 1]
    full = jnp.concatenate([fwd, jnp.zeros((1, HY_CH), F32), bwd[:0:-1]], axis=0)
    return jnp.fft.rfft(full, axis=0)


def _hyena(z, P):
    B, L, _ = z.shape
    z = _short_conv(z, P['hy_conv_w'], P['hy_conv_b'])
    x0, x1, v = jnp.split(z, 3, axis=-1)
    u = (x1 * v).astype(F32)
    Hf = _hyena_filter_fft(L, P['hy_fw1'], P['hy_fb1'], P['hy_freq1'], P['hy_fw2'],
                           P['hy_fb2'], P['hy_freq2'], P['hy_fw3'], P['hy_decay'])
    U = jnp.fft.rfft(u, n=2 * L, axis=1)
    conv = jnp.fft.irfft(U * Hf[None], n=2 * L, axis=1)[:, :L]
    return x0 * (conv + u * P['hy_skip'])


def _chunk_gmlp(z, g, ws, bs):
    B, L, _ = z.shape
    u, v = jnp.split(jax.nn.gelu(z), 2, axis=-1)
    v = _rmsnorm(v, g).reshape(B, L // CHUNK, CHUNK, SG_GROUPS, SG_GROUP_W)
    v = jnp.einsum('hpq,bnqhc->bnphc', ws, v) + bs.T[None, None, :, :, None]
    return u * v.reshape(B, L, SG_W)


N_EXPERTS = N_KEYS * N_KEYS
NEG_INF = float('-inf')
_PAIRS = [(i, j) for i in range(PEER_TOPK) for j in range(PEER_TOPK) if (i + 1) * (j + 1) <= PEER_TOPK]
_NT = (((1,), (1,)), ((), ()))


def _router_kernel(x_ref, sh_ref, sc_ref, g_ref, wqt_ref, keys_ref,
                   hb_ref, cnt1_ref, e1n_ref, rk2_ref, e2_ref,
                   q_scr, s_scr, r_scr, a_scr, c_scr, z_scr):
    tR = x_ref.shape[0]
    x = x_ref[...]
    y = x * lax.rsqrt(jnp.mean(x * x, axis=-1, keepdims=True) + EPS) * g_ref[...]
    hb = (y * (1.0 + sc_ref[0]) + sh_ref[0]).astype(BF16)
    hb_ref[...] = hb
    q_scr[...] = lax.dot_general(wqt_ref[...], hb, _NT, preferred_element_type=F32).astype(BF16)
    iota_n = lax.broadcasted_iota(jnp.int32, (N_KEYS, tR), 0).astype(F32)

    iota_r = lax.broadcasted_iota(jnp.int32, (PEER_TOPK, tR), 0).astype(F32)

    def topk_round(r, carry):
        s, rank, tops = carry
        rf = r.astype(F32)
        m = jnp.max(s, axis=0, keepdims=True)
        idx = jnp.min(jnp.where(s == m, iota_n, float(N_KEYS)), axis=0, keepdims=True)
        sel = iota_n == idx
        return jnp.where(sel, NEG_INF, s), jnp.where(sel, rf, rank), jnp.where(iota_r == rf, m, tops)

    for hp in range(2 * PEER_HEADS):
        qh = q_scr[hp * PEER_HALF:(hp + 1) * PEER_HALF, :]
        s = jnp.dot(keys_ref[hp], qh, preferred_element_type=F32)
        s_scr[hp] = s
        _, rank, tops = lax.fori_loop(0, PEER_TOPK, topk_round,
                                      (s, jnp.full((N_KEYS, tR), float(PEER_TOPK), F32),
                                       jnp.zeros((PEER_TOPK, tR), F32)))
        r_scr[hp] = rank
        base = ((hp % 2) * PEER_HEADS + hp // 2) * PEER_TOPK
        for c in range(tR // 128):
            a_scr[c, base:base + PEER_TOPK, :] = tops[:, c * 128:(c + 1) * 128]

    def a_rows(start, size, stride):
        return jnp.concatenate([a_scr[c, pl.ds(start, size, stride=stride), :] for c in range(tR // 128)], axis=1)

    a1 = [a_rows(i, PEER_HEADS, PEER_TOPK) for i in range(PEER_TOPK)]
    a2 = [a_rows(PEER_HEADS * PEER_TOPK + j, PEER_HEADS, PEER_TOPK) for j in range(PEER_TOPK)]
    vals = [a1[i] + a2[j] for (i, j) in _PAIRS]
    flat = [float(i * PEER_TOPK + j) for (i, j) in _PAIRS]
    picked = [jnp.zeros((PEER_HEADS, tR), F32) for _ in _PAIRS]
    for r in range(PEER_TOPK):
        m = functools.reduce(jnp.maximum, vals)
        idx = functools.reduce(jnp.minimum, [jnp.where(v == m, f, 1e9) for v, f in zip(vals, flat)])
        for k in range(len(_PAIRS)):
            sel = idx == flat[k]
            vals[k] = jnp.where(sel, NEG_INF, vals[k])
            picked[k] = jnp.where(sel, 1.0, picked[k])
    e1 = [jnp.exp(a1[i] - a1[0]) for i in range(PEER_TOPK)]
    e2 = [jnp.exp(a2[j] - a2[0]) for j in range(PEER_TOPK)]
    z = jnp.zeros((PEER_HEADS, tR), F32)
    for i in range(PEER_TOPK):
        row = [k for k, (pi, _) in enumerate(_PAIRS) if pi == i]
        c_scr[i] = functools.reduce(lambda a, b: a + b, [picked[k] for k in row])
        z = z + e1[i] * functools.reduce(lambda a, b: a + b, [picked[k] * e2[_PAIRS[k][1]] for k in row])
    z_scr[...] = 1.0 / z

    for h in range(PEER_HEADS):
        rank1 = r_scr[2 * h]
        rank2 = r_scr[2 * h + 1]
        cnt = jnp.zeros((N_KEYS, tR), F32)
        for r in range(PEER_TOPK):
            cnt = jnp.where(rank1 == float(r), c_scr[r, h:h + 1, :], cnt)
        cnt1_ref[h] = cnt
        top1 = a1[0][h:h + 1]
        top2 = a2[0][h:h + 1]
        e1d = jnp.exp(s_scr[2 * h] - top1) * z_scr[h:h + 1, :]
        e1n_ref[h] = jnp.where(rank1 < float(PEER_TOPK), e1d, 0.0)
        rk2_ref[h] = rank2
        e2_ref[h] = jnp.where(rank2 < float(PEER_TOPK), jnp.exp(s_scr[2 * h + 1] - top2), 0.0)


def _router(x, sh, sc, g, wqt, keys, tR=256):
    T, D = x.shape
    seg_len = T // sh.shape[0]
    seg = lambda i: ((i * tR) // seg_len, 0, 0)
    gshape = jax.ShapeDtypeStruct((PEER_HEADS, N_KEYS, T), F32)
    gspec = pl.BlockSpec((PEER_HEADS, N_KEYS, tR), lambda i: (0, 0, i))
    return pl.pallas_call(
        _router_kernel,
        grid=(T // tR,),
        in_specs=[pl.BlockSpec((tR, D), lambda i: (i, 0)),
                  pl.BlockSpec((1, 1, D), seg),
                  pl.BlockSpec((1, 1, D), seg),
                  pl.BlockSpec((1, D), lambda i: (0, 0)),
                  pl.BlockSpec(wqt.shape, lambda i: (0, 0)),
                  pl.BlockSpec(keys.shape, lambda i: (0, 0, 0))],
        out_specs=[pl.BlockSpec((tR, D), lambda i: (i, 0)), gspec, gspec, gspec, gspec],
        out_shape=[jax.ShapeDtypeStruct((T, D), BF16), gshape, gshape, gshape, gshape],
        scratch_shapes=[pltpu.VMEM((PEER_HEADS * PEER_KEY_DIM, tR), BF16),
                        pltpu.VMEM((2 * PEER_HEADS, N_KEYS, tR), F32),
                        pltpu.VMEM((2 * PEER_HEADS, N_KEYS, tR), F32),
                        pltpu.VMEM((tR // 128, 2 * PEER_HEADS * PEER_TOPK, 128), F32),
                        pltpu.VMEM((PEER_TOPK, PEER_HEADS, tR), F32),
                        pltpu.VMEM((PEER_HEADS, tR), F32)],
        compiler_params=pltpu.CompilerParams(dimension_semantics=("parallel",)),
        name="peer_router",
    )(x, sh, sc, g, wqt, keys)


def _experts_kernel(hb_ref, u_ref, vt_ref, cnt1_ref, e1n_ref, rk2_ref, e2_ref, x_ref, gf_ref,
                    o_ref, acc_ref, w_ref):
    i = pl.program_id(1)
    eT, tT = w_ref.shape
    rows = eT // N_KEYS

    @pl.when(i == 0)
    def _():
        acc_ref[...] = jnp.zeros_like(acc_ref)

    a_t = lax.dot_general(u_ref[...], hb_ref[...], _NT, preferred_element_type=F32)
    row0 = pl.multiple_of(i * rows, rows)
    for c in range(tT // 128):
        cols = slice(c * 128, (c + 1) * 128)
        cnt = [cnt1_ref[h, pl.ds(row0, rows), cols] for h in range(PEER_HEADS)]
        e1 = [e1n_ref[h, pl.ds(row0, rows), cols] for h in range(PEER_HEADS)]
        for r in range(rows):
            g = jnp.zeros((N_KEYS, 128), F32)
            for h in range(PEER_HEADS):
                g = g + jnp.where(rk2_ref[h, :, cols] < cnt[h][r:r + 1], e2_ref[h, :, cols] * e1[h][r:r + 1], 0.0)
            a = a_t[r * N_KEYS:(r + 1) * N_KEYS, cols]
            w_ref[r * N_KEYS:(r + 1) * N_KEYS, cols] = (g * jax.nn.gelu(a)).astype(BF16)
    acc_ref[...] += jnp.dot(vt_ref[...], w_ref[...], preferred_element_type=F32)

    @pl.when(i == pl.num_programs(1) - 1)
    def _():
        o_ref[...] = x_ref[...] + gf_ref[0] * acc_ref[...].T


def _experts(hb, u_bf, vt_bf, cnt1, e1n, rk2, e2, x, gf, tT=512, eT=1024):
    T, D = x.shape
    seg_len = T // gf.shape[0]
    gspec = pl.BlockSpec((PEER_HEADS, N_KEYS, tT), lambda j, i: (0, 0, j))
    return pl.pallas_call(
        _experts_kernel,
        grid=(T // tT, N_EXPERTS // eT),
        in_specs=[pl.BlockSpec((tT, D), lambda j, i: (j, 0)),
                  pl.BlockSpec((eT, D), lambda j, i: (i, 0)),
                  pl.BlockSpec((D, eT), lambda j, i: (0, i)),
                  gspec, gspec, gspec, gspec,
                  pl.BlockSpec((tT, D), lambda j, i: (j, 0)),
                  pl.BlockSpec((1, 1, D), lambda j, i: ((j * tT) // seg_len, 0, 0))],
        out_specs=pl.BlockSpec((tT, D), lambda j, i: (j, 0)),
        out_shape=jax.ShapeDtypeStruct((T, D), F32),
        scratch_shapes=[pltpu.VMEM((D, tT), F32), pltpu.VMEM((eT, tT), BF16)],
        compiler_params=pltpu.CompilerParams(dimension_semantics=("parallel", "arbitrary"),
                                             vmem_limit_bytes=48 * 1024 * 1024),
        name="peer_experts",
    )(hb, u_bf, vt_bf, cnt1, e1n, rk2, e2, x, gf)


def _peer_block(x, sh_f, sc_f, g_f, P):
    B, L, D = x.shape
    xf = x.reshape(B * L, D)
    hb, cnt1, e1n, rk2, e2 = _router(xf, sh_f, sc_f, P['norm_ffn_g'].reshape(1, D), P['peer_wqt'], P['peer_keys_bf'])
    out = _experts(hb, P['peer_u_bf'], P['peer_vt_bf'], cnt1, e1n, rk2, e2, xf, g_f)
    return out.reshape(B, L, D)


def _layer(x, mod, P, rope, k_ext, v_ext):
    B, L, D = x.shape
    sh_m, sc_m, g_m, sh_f, sc_f, g_f = jnp.split(mod, N_MOD, axis=-1)
    h = _rmsnorm(x, P['norm_mix_g']) * (1.0 + sc_m) + sh_m
    proj = _mm(h.reshape(B * L, D), P['w_in']).reshape(B, L, -1)
    q = _rmsnorm(proj[..., :Q_END].reshape(B, L, N_HEADS, HEAD_DIM), P['q_norm_g'])
    k = _rmsnorm(proj[..., Q_END:K_END].reshape(B, L, N_KV_HEADS, HEAD_DIM), P['k_norm_g'])
    v = proj[..., K_END:V_END].reshape(B, L, N_KV_HEADS, HEAD_DIM)
    if rope is None:
        attn = _attention(q, k, v)
    else:
        cos, sin = rope
        q_r = _apply_rope(q, cos, sin)
        k_r = _apply_rope(k, cos, sin)
        k_all = jnp.concatenate([k_ext, k_r], axis=1)
        v_all = jnp.concatenate([v_ext, v], axis=1)
        attn = _attention(q_r, k_all, v_all)
    hy = _hyena(proj[..., V_END:HY_END], P)
    sg = _chunk_gmlp(proj[..., HY_END:], P['sg_norm_g'], P['sg_w'], P['sg_b'])
    cat = jnp.concatenate([attn, hy, sg], axis=-1)
    mixed = _mm(cat.reshape(B * L, -1), P['w_out']).reshape(B, L, D)
    x = x + g_m * mixed
    x = _peer_block(x, sh_f, sc_f, g_f, P)
    return x, k, v


def kernel(x_prompt, x_sample, cache_k, cache_v, c, c_ctx, norm_mix_g, norm_ffn_g, ada_w, ada_b,
           w_in, w_out, q_norm_g, k_norm_g, hy_conv_w, hy_conv_b, hy_fw1, hy_fb1, hy_freq1,
           hy_fw2, hy_fb2, hy_freq2, hy_fw3, hy_decay, hy_skip, sg_norm_g, sg_w, sg_b,
           peer_wq, peer_keys, peer_u, peer_v):
    rope = _rope_tables(x_sample.shape[1])
    y_p = x_prompt
    y_s = x_sample
    ks_ctx = []
    vs_ctx = []
    for l in range(DEPTH):
        P = {
            'norm_mix_g': norm_mix_g[l], 'norm_ffn_g': norm_ffn_g[l],
            'w_in': w_in[l], 'w_out': w_out[l],
            'q_norm_g': q_norm_g[l], 'k_norm_g': k_norm_g[l],
            'hy_conv_w': hy_conv_w[l], 'hy_conv_b': hy_conv_b[l],
            'hy_fw1': hy_fw1[l], 'hy_fb1': hy_fb1[l], 'hy_freq1': hy_freq1[l],
            'hy_fw2': hy_fw2[l], 'hy_fb2': hy_fb2[l], 'hy_freq2': hy_freq2[l],
            'hy_fw3': hy_fw3[l], 'hy_decay': hy_decay[l], 'hy_skip': hy_skip[l],
            'sg_norm_g': sg_norm_g[l], 'sg_w': sg_w[l], 'sg_b': sg_b[l],
            'peer_wqt': peer_wq[l].T.astype(BF16),
            'peer_keys_bf': peer_keys[l].reshape(2 * PEER_HEADS, N_KEYS, PEER_HALF).astype(BF16),
            'peer_u_bf': peer_u[l].astype(BF16), 'peer_vt_bf': peer_v[l].T.astype(BF16),
        }
        mod_ctx = (jax.nn.silu(c_ctx)[None, :] @ ada_w[l] + ada_b[l])[:, None, :]
        mod_lat = (jax.nn.silu(c) @ ada_w[l] + ada_b[l])[:, None, :]
        y_p, k_l, v_l = _layer(y_p, mod_ctx, P, None, None, None)
        ks_ctx.append(k_l)
        vs_ctx.append(v_l)
        y_s, _, _ = _layer(y_s, mod_lat, P, rope, cache_k[:, l], cache_v[:, l])
    new_cache_k = jnp.stack(ks_ctx, axis=1)
    new_cache_v = jnp.stack(vs_ctx, axis=1)
    return (y_p, y_s, new_cache_k, new_cache_v)
```

```python
import functools

import jax
import jax.numpy as jnp
from jax import lax
from jax.experimental import pallas as pl
from jax.experimental.pallas import tpu as pltpu

D_MODEL = 1024
DEPTH = 2
GRID_W = 64
N_MOD = 6
EPS = 1e-6
N_HEADS = 8
N_KV_HEADS = 2
HEAD_DIM = 64
ATTN_W = N_HEADS * HEAD_DIM
KV_W = N_KV_HEADS * HEAD_DIM
Q_BLOCK = 128
ROPE_THETA = 10000.0
HY_CH = 256
HY_BANDS = 16
SG_W = 256
SG_GROUPS = 4
SG_GROUP_W = SG_W // SG_GROUPS
CHUNK = 128
Q_END = ATTN_W
K_END = Q_END + KV_W
V_END = K_END + KV_W
HY_END = V_END + 3 * HY_CH
PEER_HEADS = 8
N_KEYS = 128
PEER_KEY_DIM = 128
PEER_HALF = PEER_KEY_DIM // 2
PEER_TOPK = 16
TOKEN_BLOCK = 128

F32 = jnp.float32
BF16 = jnp.bfloat16


def _mm_kernel(a_ref, b_ref, o_ref):
    o_ref[...] = jnp.dot(a_ref[...].astype(BF16), b_ref[...].astype(BF16),
                         preferred_element_type=F32)


def _mm(a, b, tm=512, tn=512):
    M, K = a.shape
    _, N = b.shape
    tm = min(tm, M)
    tn = min(tn, N)
    return pl.pallas_call(
        _mm_kernel,
        grid=(M // tm, N // tn),
        in_specs=[pl.BlockSpec((tm, K), lambda i, j: (i, 0)),
                  pl.BlockSpec((K, tn), lambda i, j: (0, j))],
        out_specs=pl.BlockSpec((tm, tn), lambda i, j: (i, j)),
        out_shape=jax.ShapeDtypeStruct((M, N), F32),
    )(a, b)


def _rmsnorm(x, g):
    xf = x.astype(F32)
    y = xf * lax.rsqrt(jnp.mean(xf * xf, axis=-1, keepdims=True) + EPS)
    return (y * g.astype(F32)).astype(x.dtype)


def _rope_tables(L):
    rows = L // GRID_W
    row = jnp.broadcast_to(jnp.arange(rows, dtype=F32)[:, None], (rows, GRID_W)).reshape(L)
    col = jnp.broadcast_to(jnp.arange(GRID_W, dtype=F32)[None, :], (rows, GRID_W)).reshape(L)
    n_axis = HEAD_DIM // 4
    inv = ROPE_THETA ** (-jnp.arange(n_axis, dtype=F32) / n_axis)
    ang = jnp.concatenate([row[:, None] * inv, col[:, None] * inv], axis=-1)
    return jnp.cos(ang), jnp.sin(ang)


def _apply_rope(x, cos, sin):
    B, L, H, _ = x.shape
    xp = x.astype(F32).reshape(B, L, H, HEAD_DIM // 2, 2)
    a, b = xp[..., 0], xp[..., 1]
    c = cos[None, :, None, :]
    s = sin[None, :, None, :]
    out = jnp.stack([a * c - b * s, a * s + b * c], axis=-1).reshape(B, L, H, HEAD_DIM)
    return out.astype(x.dtype)


GROUP = N_HEADS // N_KV_HEADS
_NT = (((1,), (1,)), ((), ()))


def _attn_kernel(q_ref, k_ref, v_ref, o_ref):
    k = k_ref[0, 0]
    v = v_ref[0, 0]
    outs = []
    for g in range(GROUP):
        qg = q_ref[0, :, g * HEAD_DIM:(g + 1) * HEAD_DIM]
        s = lax.dot_general(qg, k, _NT, preferred_element_type=F32) * (HEAD_DIM ** -0.5)
        p = jnp.exp(s - jnp.max(s, axis=-1, keepdims=True))
        l = jnp.sum(p, axis=-1, keepdims=True)
        outs.append(jnp.dot(p.astype(BF16), v, preferred_element_type=F32) / l)
    o_ref[0] = jnp.concatenate(outs, axis=-1)


def _attention(q, k, v, tq=256):
    B, Lq, H, Dh = q.shape
    Lk = k.shape[1]
    qf = q.reshape(B, Lq, H * Dh).astype(BF16)
    kt = k.transpose(0, 2, 1, 3).astype(BF16)
    vt = v.transpose(0, 2, 1, 3).astype(BF16)
    return pl.pallas_call(
        _attn_kernel,
        grid=(B, N_KV_HEADS, Lq // tq),
        in_specs=[pl.BlockSpec((1, tq, GROUP * Dh), lambda b, kh, i: (b, i, kh)),
                  pl.BlockSpec((1, 1, Lk, Dh), lambda b, kh, i: (b, kh, 0, 0)),
                  pl.BlockSpec((1, 1, Lk, Dh), lambda b, kh, i: (b, kh, 0, 0))],
        out_specs=pl.BlockSpec((1, tq, GROUP * Dh), lambda b, kh, i: (b, i, kh)),
        out_shape=jax.ShapeDtypeStruct((B, Lq, H * Dh), F32),
        compiler_params=pltpu.CompilerParams(dimension_semantics=("parallel", "parallel", "parallel"),
                                             vmem_limit_bytes=48 * 1024 * 1024),
        name="attention",
    )(qf, kt, vt)


def _short_conv(z, w, b):
    zp = jnp.pad(z, ((0, 0), (1, 1), (0, 0)))
    return zp[:, :-2] * w[0] + zp[:, 1:-1] * w[1] + zp[:, 2:] * w[2] + b


def _hyena_filter_fft(L, fw1, fb1, fr1, fw2, fb2, fr2, fw3, decay):
    t = jnp.arange(L, dtype=F32)
    tn = t / (L - 1)
    w = 2.0 * jnp.pi * t / L
    f = jnp.linspace(1e-4, HY_BANDS - 1, HY_BANDS, dtype=F32)
    zf = w[:, None] * f[None, :]
    feats = jnp.concatenate([tn[:, None], jnp.cos(zf), -jnp.sin(zf)], axis=-1)
    h = jnp.sin(fr1 * (feats @ fw1 + fb1))
    h = jnp.sin(fr2 * (h @ fw2 + fb2))
    h = (h @ fw3).reshape(L, 2, HY_CH)
    h = h * jnp.exp(-tn[:, None, None] * jnp.abs(decay))
    fwd, bwd = h[:, 0], h[:, 1]
    full = jnp.concatenate([fwd, jnp.zeros((1, HY_CH), F32), bwd[:0:-1]], axis=0)
    return jnp.fft.rfft(full, axis=0)


def _hyena(z, P):
    B, L, _ = z.shape
    z = _short_conv(z, P['hy_conv_w'], P['hy_conv_b'])
    x0, x1, v = jnp.split(z, 3, axis=-1)
    u = (x1 * v).astype(F32)
    Hf = _hyena_filter_fft(L, P['hy_fw1'], P['hy_fb1'], P['hy_freq1'], P['hy_fw2'],
                           P['hy_fb2'], P['hy_freq2'], P['hy_fw3'], P['hy_decay'])
    U = jnp.fft.rfft(u, n=2 * L, axis=1)
    conv = jnp.fft.irfft(U * Hf[None], n=2 * L, axis=1)[:, :L]
    return x0 * (conv + u * P['hy_skip'])


def _chunk_gmlp(z, g, ws, bs):
    B, L, _ = z.shape
    u, v = jnp.split(jax.nn.gelu(z), 2, axis=-1)
    v = _rmsnorm(v, g).reshape(B, L // CHUNK, CHUNK, SG_GROUPS, SG_GROUP_W)
    v = jnp.einsum('hpq,bnqhc->bnphc', ws, v) + bs.T[None, None, :, :, None]
    return u * v.reshape(B, L, SG_W)


N_EXPERTS = N_KEYS * N_KEYS
NEG_INF = float('-inf')
_PAIRS = [(i, j) for i in range(PEER_TOPK) for j in range(PEER_TOPK) if (i + 1) * (j + 1) <= PEER_TOPK]


def _router_kernel(x_ref, sh_ref, sc_ref, g_ref, wqt_ref, keys_ref,
                   hb_ref, cnt1_ref, e1n_ref, rk2_ref, e2_ref,
                   q_scr, s_scr, r_scr, a_scr, c_scr, z_scr):
    tR = x_ref.shape[0]
    x = x_ref[...]
    y = x * lax.rsqrt(jnp.mean(x * x, axis=-1, keepdims=True) + EPS) * g_ref[...]
    hb = (y * (1.0 + sc_ref[0]) + sh_ref[0]).astype(BF16)
    hb_ref[...] = hb
    q_scr[...] = lax.dot_general(wqt_ref[...], hb, _NT, preferred_element_type=F32).astype(BF16)
    iota_n = lax.broadcasted_iota(jnp.int32, (N_KEYS, tR), 0).astype(F32)

    iota_r = lax.broadcasted_iota(jnp.int32, (PEER_TOPK, tR), 0).astype(F32)

    def topk_round(r, carry):
        s, rank, tops = carry
        rf = lax.convert_element_type(r, F32)
        m = jnp.max(s, axis=0, keepdims=True)
        idx = jnp.min(jnp.where(s == m, iota_n, float(N_KEYS)), axis=0, keepdims=True)
        sel = iota_n == idx
        return jnp.where(sel, NEG_INF, s), jnp.where(sel, rf, rank), jnp.where(iota_r == rf, m, tops)

    for hp in range(2 * PEER_HEADS):
        qh = q_scr[hp * PEER_HALF:(hp + 1) * PEER_HALF, :]
        s = jnp.dot(keys_ref[hp], qh, preferred_element_type=F32)
        s_scr[hp] = s
        _, rank, tops = lax.fori_loop(0, PEER_TOPK, topk_round,
                                      (s, jnp.full((N_KEYS, tR), float(PEER_TOPK), F32),
                                       jnp.zeros((PEER_TOPK, tR), F32)))
        r_scr[hp] = rank
        base = ((hp % 2) * PEER_HEADS + hp // 2) * PEER_TOPK
        for c in range(tR // 128):
            a_scr[c, base:base + PEER_TOPK, :] = tops[:, c * 128:(c + 1) * 128]

    def a_rows(start, size, stride):
        return jnp.concatenate([a_scr[c, pl.ds(start, size, stride=stride), :] for c in range(tR // 128)], axis=1)

    a1 = [a_rows(i, PEER_HEADS, PEER_TOPK) for i in range(PEER_TOPK)]
    a2 = [a_rows(PEER_HEADS * PEER_TOPK + j, PEER_HEADS, PEER_TOPK) for j in range(PEER_TOPK)]
    vals = [a1[i] + a2[j] for (i, j) in _PAIRS]
    flat = [float(i * PEER_TOPK + j) for (i, j) in _PAIRS]
    picked = [jnp.zeros((PEER_HEADS, tR), F32) for _ in _PAIRS]
    for r in range(PEER_TOPK):
        m = functools.reduce(jnp.maximum, vals)
        idx = functools.reduce(jnp.minimum, [jnp.where(v == m, f, 1e9) for v, f in zip(vals, flat)])
        for k in range(len(_PAIRS)):
            sel = idx == flat[k]
            vals[k] = jnp.where(sel, NEG_INF, vals[k])
            picked[k] = jnp.where(sel, 1.0, picked[k])
    e1 = [jnp.exp(a1[i] - a1[0]) for i in range(PEER_TOPK)]
    e2 = [jnp.exp(a2[j] - a2[0]) for j in range(PEER_TOPK)]
    z = jnp.zeros((PEER_HEADS, tR), F32)
    for i in range(PEER_TOPK):
        row = [k for k, (pi, _) in enumerate(_PAIRS) if pi == i]
        c_scr[i] = functools.reduce(lambda a, b: a + b, [picked[k] for k in row])
        z = z + e1[i] * functools.reduce(lambda a, b: a + b, [picked[k] * e2[_PAIRS[k][1]] for k in row])
    z_scr[...] = 1.0 / z

    for h in range(PEER_HEADS):
        rank1 = r_scr[2 * h]
        rank2 = r_scr[2 * h + 1]
        cnt = jnp.zeros((N_KEYS, tR), F32)
        for r in range(PEER_TOPK):
            cnt = jnp.where(rank1 == float(r), c_scr[r, h:h + 1, :], cnt)
        cnt1_ref[h] = cnt
        top1 = a1[0][h:h + 1]
        top2 = a2[0][h:h + 1]
        e1d = jnp.exp(s_scr[2 * h] - top1) * z_scr[h:h + 1, :]
        e1n_ref[h] = jnp.where(rank1 < float(PEER_TOPK), e1d, 0.0)
        rk2_ref[h] = rank2
        e2_ref[h] = jnp.where(rank2 < float(PEER_TOPK), jnp.exp(s_scr[2 * h + 1] - top2), 0.0)


def _router(x, sh, sc, g, wqt, keys, tR=256):
    T, D = x.shape
    seg_len = T // sh.shape[0]
    seg = lambda i: ((i * tR) // seg_len, 0, 0)
    gshape = jax.ShapeDtypeStruct((PEER_HEADS, N_KEYS, T), F32)
    gspec = pl.BlockSpec((PEER_HEADS, N_KEYS, tR), lambda i: (0, 0, i))
    return pl.pallas_call(
        _router_kernel,
        grid=(T // tR,),
        in_specs=[pl.BlockSpec((tR, D), lambda i: (i, 0)),
                  pl.BlockSpec((1, 1, D), seg),
                  pl.BlockSpec((1, 1, D), seg),
                  pl.BlockSpec((1, D), lambda i: (0, 0)),
                  pl.BlockSpec(wqt.shape, lambda i: (0, 0)),
                  pl.BlockSpec(keys.shape, lambda i: (0, 0, 0))],
        out_specs=[pl.BlockSpec((tR, D), lambda i: (i, 0)), gspec, gspec, gspec, gspec],
        out_shape=[jax.ShapeDtypeStruct((T, D), BF16), gshape, gshape, gshape, gshape],
        scratch_shapes=[pltpu.VMEM((PEER_HEADS * PEER_KEY_DIM, tR), BF16),
                        pltpu.VMEM((2 * PEER_HEADS, N_KEYS, tR), F32),
                        pltpu.VMEM((2 * PEER_HEADS, N_KEYS, tR), F32),
                        pltpu.VMEM((tR // 128, 2 * PEER_HEADS * PEER_TOPK, 128), F32),
                        pltpu.VMEM((PEER_TOPK, PEER_HEADS, tR), F32),
                        pltpu.VMEM((PEER_HEADS, tR), F32)],
        compiler_params=pltpu.CompilerParams(dimension_semantics=("parallel",)),
        name="peer_router",
    )(x, sh, sc, g, wqt, keys)


def _experts_kernel(hb_ref, u_ref, vt_ref, cnt1_ref, e1n_ref, rk2_ref, e2_ref, x_ref, gf_ref,
                    o_ref, acc_ref, w_ref):
    i = pl.program_id(1)
    eT, tT = w_ref.shape
    rows = eT // N_KEYS

    @pl.when(i == 0)
    def _():
        acc_ref[...] = jnp.zeros_like(acc_ref)

    a_t = lax.dot_general(u_ref[...], hb_ref[...], _NT, preferred_element_type=F32)
    row0 = pl.multiple_of(i * rows, rows)
    for c in range(tT // 128):
        cols = slice(c * 128, (c + 1) * 128)
        cnt = [cnt1_ref[h, pl.ds(row0, rows), cols] for h in range(PEER_HEADS)]
        e1 = [e1n_ref[h, pl.ds(row0, rows), cols] for h in range(PEER_HEADS)]
        for r in range(rows):
            g = jnp.zeros((N_KEYS, 128), F32)
            for h in range(PEER_HEADS):
                g = g + jnp.where(rk2_ref[h, :, cols] < cnt[h][r:r + 1], e2_ref[h, :, cols] * e1[h][r:r + 1], 0.0)
            a = a_t[r * N_KEYS:(r + 1) * N_KEYS, cols]
            w_ref[r * N_KEYS:(r + 1) * N_KEYS, cols] = (g * jax.nn.gelu(a)).astype(BF16)
    acc_ref[...] += jnp.dot(vt_ref[...], w_ref[...], preferred_element_type=F32)

    @pl.when(i == pl.num_programs(1) - 1)
    def _():
        o_ref[...] = x_ref[...] + gf_ref[0] * acc_ref[...].T


def _experts(hb, u_bf, vt_bf, cnt1, e1n, rk2, e2, x, gf, tT=512, eT=1024):
    T, D = x.shape
    seg_len = T // gf.shape[0]
    gspec = pl.BlockSpec((PEER_HEADS, N_KEYS, tT), lambda j, i: (0, 0, j))
    return pl.pallas_call(
        _experts_kernel,
        grid=(T // tT, N_EXPERTS // eT),
        in_specs=[pl.BlockSpec((tT, D), lambda j, i: (j, 0)),
                  pl.BlockSpec((eT, D), lambda j, i: (i, 0)),
                  pl.BlockSpec((D, eT), lambda j, i: (0, i)),
                  gspec, gspec, gspec, gspec,
                  pl.BlockSpec((tT, D), lambda j, i: (j, 0)),
                  pl.BlockSpec((1, 1, D), lambda j, i: ((j * tT) // seg_len, 0, 0))],
        out_specs=pl.BlockSpec((tT, D), lambda j, i: (j, 0)),
        out_shape=jax.ShapeDtypeStruct((T, D), F32),
        scratch_shapes=[pltpu.VMEM((D, tT), F32), pltpu.VMEM((eT, tT), BF16)],
        compiler_params=pltpu.CompilerParams(dimension_semantics=("parallel", "arbitrary"),
                                             vmem_limit_bytes=48 * 1024 * 1024),
        name="peer_experts",
    )(hb, u_bf, vt_bf, cnt1, e1n, rk2, e2, x, gf)


def _peer_block(x, sh_f, sc_f, g_f, P):
    B, L, D = x.shape
    xf = x.reshape(B * L, D)
    hb, cnt1, e1n, rk2, e2 = _router(xf, sh_f, sc_f, P['norm_ffn_g'].reshape(1, D), P['peer_wqt'], P['peer_keys_bf'])
    out = _experts(hb, P['peer_u_bf'], P['peer_vt_bf'], cnt1, e1n, rk2, e2, xf, g_f)
    return out.reshape(B, L, D)


def _layer(x, mod, P, rope, k_ext, v_ext):
    B, L, D = x.shape
    sh_m, sc_m, g_m, sh_f, sc_f, g_f = jnp.split(mod, N_MOD, axis=-1)
    h = _rmsnorm(x, P['norm_mix_g']) * (1.0 + sc_m) + sh_m
    proj = _mm(h.reshape(B * L, D), P['w_in']).reshape(B, L, -1)
    q = _rmsnorm(proj[..., :Q_END].reshape(B, L, N_HEADS, HEAD_DIM), P['q_norm_g'])
    k = _rmsnorm(proj[..., Q_END:K_END].reshape(B, L, N_KV_HEADS, HEAD_DIM), P['k_norm_g'])
    v = proj[..., K_END:V_END].reshape(B, L, N_KV_HEADS, HEAD_DIM)
    if rope is None:
        attn = _attention(q, k, v)
    else:
        cos, sin = rope
        q_r = _apply_rope(q, cos, sin)
        k_r = _apply_rope(k, cos, sin)
        k_all = jnp.concatenate([k_ext, k_r], axis=1)
        v_all = jnp.concatenate([v_ext, v], axis=1)
        attn = _attention(q_r, k_all, v_all)
    hy = _hyena(proj[..., V_END:HY_END], P)
    sg = _chunk_gmlp(proj[..., HY_END:], P['sg_norm_g'], P['sg_w'], P['sg_b'])
    cat = jnp.concatenate([attn, hy, sg], axis=-1)
    mixed = _mm(cat.reshape(B * L, -1), P['w_out']).reshape(B, L, D)
    x = x + g_m * mixed
    x = _peer_block(x, sh_f, sc_f, g_f, P)
    return x, k, v


def kernel(x_prompt, x_sample, cache_k, cache_v, c, c_ctx, norm_mix_g, norm_ffn_g, ada_w, ada_b,
           w_in, w_out, q_norm_g, k_norm_g, hy_conv_w, hy_conv_b, hy_fw1, hy_fb1, hy_freq1,
           hy_fw2, hy_fb2, hy_freq2, hy_fw3, hy_decay, hy_skip, sg_norm_g, sg_w, sg_b,
           peer_wq, peer_keys, peer_u, peer_v):
    rope = _rope_tables(x_sample.shape[1])
    y_p = x_prompt
    y_s = x_sample
    ks_ctx = []
    vs_ctx = []
    for l in range(DEPTH):
        P = {
            'norm_mix_g': norm_mix_g[l], 'norm_ffn_g': norm_ffn_g[l],
            'w_in': w_in[l], 'w_out': w_out[l],
            'q_norm_g': q_norm_g[l], 'k_norm_g': k_norm_g[l],
            'hy_conv_w': hy_conv_w[l], 'hy_conv_b': hy_conv_b[l],
            'hy_fw1': hy_fw1[l], 'hy_fb1': hy_fb1[l], 'hy_freq1': hy_freq1[l],
            'hy_fw2': hy_fw2[l], 'hy_fb2': hy_fb2[l], 'hy_freq2': hy_freq2[l],
            'hy_fw3': hy_fw3[l], 'hy_decay': hy_decay[l], 'hy_skip': hy_skip[l],
            'sg_norm_g': sg_norm_g[l], 'sg_w': sg_w[l], 'sg_b': sg_b[l],
            'peer_wqt': peer_wq[l].T.astype(BF16),
            'peer_keys_bf': peer_keys[l].reshape(2 * PEER_HEADS, N_KEYS, PEER_HALF).astype(BF16),
            'peer_u_bf': peer_u[l].astype(BF16), 'peer_vt_bf': peer_v[l].T.astype(BF16),
        }
        mod_ctx = (jax.nn.silu(c_ctx)[None, :] @ ada_w[l] + ada_b[l])[:, None, :]
        mod_lat = (jax.nn.silu(c) @ ada_w[l] + ada_b[l])[:, None, :]
        y_p, k_l, v_l = _layer(y_p, mod_ctx, P, None, None, None)
        ks_ctx.append(k_l)
        vs_ctx.append(v_l)
        y_s, _, _ = _layer(y_s, mod_lat, P, rope, cache_k[:, l], cache_v[:, l])
    new_cache_k = jnp.stack(ks_ctx, axis=1)
    new_cache_v = jnp.stack(vs_ctx, axis=1)
    return (y_p, y_s, new_cache_k, new_cache_v)
```

```python
import functools

import jax
import jax.numpy as jnp
from jax import lax
from jax.experimental import pallas as pl
from jax.experimental.pallas import tpu as pltpu

D_MODEL = 1024
DEPTH = 2
GRID_W = 64
N_MOD = 6
EPS = 1e-6
N_HEADS = 8
N_KV_HEADS = 2
HEAD_DIM = 64
ATTN_W = N_HEADS * HEAD_DIM
KV_W = N_KV_HEADS * HEAD_DIM
Q_BLOCK = 128
ROPE_THETA = 10000.0
HY_CH = 256
HY_BANDS = 16
SG_W = 256
SG_GROUPS = 4
SG_GROUP_W = SG_W // SG_GROUPS
CHUNK = 128
Q_END = ATTN_W
K_END = Q_END + KV_W
V_END = K_END + KV_W
HY_END = V_END + 3 * HY_CH
PEER_HEADS = 8
N_KEYS = 128
PEER_KEY_DIM = 128
PEER_HALF = PEER_KEY_DIM // 2
PEER_TOPK = 16
TOKEN_BLOCK = 128

F32 = jnp.float32
BF16 = jnp.bfloat16


def _mm_kernel(a_ref, b_ref, o_ref):
    o_ref[...] = jnp.dot(a_ref[...].astype(BF16), b_ref[...].astype(BF16),
                         preferred_element_type=F32)


def _mm(a, b, tm=512, tn=512):
    M, K = a.shape
    _, N = b.shape
    tm = min(tm, M)
    tn = min(tn, N)
    return pl.pallas_call(
        _mm_kernel,
        grid=(M // tm, N // tn),
        in_specs=[pl.BlockSpec((tm, K), lambda i, j: (i, 0)),
                  pl.BlockSpec((K, tn), lambda i, j: (0, j))],
        out_specs=pl.BlockSpec((tm, tn), lambda i, j: (i, j)),
        out_shape=jax.ShapeDtypeStruct((M, N), F32),
    )(a, b)


def _rmsnorm(x, g):
    xf = x.astype(F32)
    y = xf * lax.rsqrt(jnp.mean(xf * xf, axis=-1, keepdims=True) + EPS)
    return (y * g.astype(F32)).astype(x.dtype)


def _rope_tables(L):
    rows = L // GRID_W
    row = jnp.broadcast_to(jnp.arange(rows, dtype=F32)[:, None], (rows, GRID_W)).reshape(L)
    col = jnp.broadcast_to(jnp.arange(GRID_W, dtype=F32)[None, :], (rows, GRID_W)).reshape(L)
    n_axis = HEAD_DIM // 4
    inv = ROPE_THETA ** (-jnp.arange(n_axis, dtype=F32) / n_axis)
    ang = jnp.concatenate([row[:, None] * inv, col[:, None] * inv], axis=-1)
    return jnp.cos(ang), jnp.sin(ang)


def _apply_rope(x, cos, sin):
    B, L, H, _ = x.shape
    xp = x.astype(F32).reshape(B, L, H, HEAD_DIM // 2, 2)
    a, b = xp[..., 0], xp[..., 1]
    c = cos[None, :, None, :]
    s = sin[None, :, None, :]
    out = jnp.stack([a * c - b * s, a * s + b * c], axis=-1).reshape(B, L, H, HEAD_DIM)
    return out.astype(x.dtype)


GROUP = N_HEADS // N_KV_HEADS
_NT = (((1,), (1,)), ((), ()))


def _attn_kernel(q_ref, k_ref, v_ref, o_ref):
    k = k_ref[0, 0]
    v = v_ref[0, 0]
    outs = []
    for g in range(GROUP):
        qg = q_ref[0, :, g * HEAD_DIM:(g + 1) * HEAD_DIM]
        s = lax.dot_general(qg, k, _NT, preferred_element_type=F32) * (HEAD_DIM ** -0.5)
        p = jnp.exp(s - jnp.max(s, axis=-1, keepdims=True))
        l = jnp.sum(p, axis=-1, keepdims=True)
        outs.append(jnp.dot(p.astype(BF16), v, preferred_element_type=F32) / l)
    o_ref[0] = jnp.concatenate(outs, axis=-1)


def _attention(q, k, v, tq=256):
    B, Lq, H, Dh = q.shape
    Lk = k.shape[1]
    qf = q.reshape(B, Lq, H * Dh).astype(BF16)
    kt = k.transpose(0, 2, 1, 3).astype(BF16)
    vt = v.transpose(0, 2, 1, 3).astype(BF16)
    return pl.pallas_call(
        _attn_kernel,
        grid=(B, N_KV_HEADS, Lq // tq),
        in_specs=[pl.BlockSpec((1, tq, GROUP * Dh), lambda b, kh, i: (b, i, kh)),
                  pl.BlockSpec((1, 1, Lk, Dh), lambda b, kh, i: (b, kh, 0, 0)),
                  pl.BlockSpec((1, 1, Lk, Dh), lambda b, kh, i: (b, kh, 0, 0))],
        out_specs=pl.BlockSpec((1, tq, GROUP * Dh), lambda b, kh, i: (b, i, kh)),
        out_shape=jax.ShapeDtypeStruct((B, Lq, H * Dh), F32),
        compiler_params=pltpu.CompilerParams(dimension_semantics=("parallel", "parallel", "parallel"),
                                             vmem_limit_bytes=48 * 1024 * 1024),
        name="attention",
    )(qf, kt, vt)


def _short_conv(z, w, b):
    zp = jnp.pad(z, ((0, 0), (1, 1), (0, 0)))
    return zp[:, :-2] * w[0] + zp[:, 1:-1] * w[1] + zp[:, 2:] * w[2] + b


def _hyena_filter_fft(L, fw1, fb1, fr1, fw2, fb2, fr2, fw3, decay):
    t = jnp.arange(L, dtype=F32)
    tn = t / (L - 1)
    w = 2.0 * jnp.pi * t / L
    f = jnp.linspace(1e-4, HY_BANDS - 1, HY_BANDS, dtype=F32)
    zf = w[:, None] * f[None, :]
    feats = jnp.concatenate([tn[:, None], jnp.cos(zf), -jnp.sin(zf)], axis=-1)
    h = jnp.sin(fr1 * (feats @ fw1 + fb1))
    h = jnp.sin(fr2 * (h @ fw2 + fb2))
    h = (h @ fw3).reshape(L, 2, HY_CH)
    h = h * jnp.exp(-tn[:, None, None] * jnp.abs(decay))
    fwd, bwd = h[:, 0], h[:, 1]
    full = jnp.concatenate([fwd, jnp.zeros((1, HY_CH), F32), bwd[:0:-1]], axis=0)
    return jnp.fft.rfft(full, axis=0)


def _hyena(z, P):
    B, L, _ = z.shape
    z = _short_conv(z, P['hy_conv_w'], P['hy_conv_b'])
    x0, x1, v = jnp.split(z, 3, axis=-1)
    u = (x1 * v).astype(F32)
    Hf = _hyena_filter_fft(L, P['hy_fw1'], P['hy_fb1'], P['hy_freq1'], P['hy_fw2'],
                           P['hy_fb2'], P['hy_freq2'], P['hy_fw3'], P['hy_decay'])
    U = jnp.fft.rfft(u, n=2 * L, axis=1)
    conv = jnp.fft.irfft(U * Hf[None], n=2 * L, axis=1)[:, :L]
    return x0 * (conv + u * P['hy_skip'])


def _chunk_gmlp(z, g, ws, bs):
    B, L, _ = z.shape
    u, v = jnp.split(jax.nn.gelu(z), 2, axis=-1)
    v = _rmsnorm(v, g).reshape(B, L // CHUNK, CHUNK, SG_GROUPS, SG_GROUP_W)
    v = jnp.einsum('hpq,bnqhc->bnphc', ws, v) + bs.T[None, None, :, :, None]
    return u * v.reshape(B, L, SG_W)


N_EXPERTS = N_KEYS * N_KEYS
NEG_INF = float('-inf')
_PAIRS = [(i, j) for i in range(PEER_TOPK) for j in range(PEER_TOPK) if (i + 1) * (j + 1) <= PEER_TOPK]


def _router_kernel(x_ref, sh_ref, sc_ref, g_ref, wqt_ref, keys_ref,
                   hb_ref, cnt1_ref, e1n_ref, rk2_ref, e2_ref,
                   q_scr, s_scr, r_scr, a_scr, c_scr, z_scr):
    tR = x_ref.shape[0]
    x = x_ref[...]
    y = x * lax.rsqrt(jnp.mean(x * x, axis=-1, keepdims=True) + EPS) * g_ref[...]
    hb = (y * (1.0 + sc_ref[0]) + sh_ref[0]).astype(BF16)
    hb_ref[...] = hb
    q_scr[...] = lax.dot_general(wqt_ref[...], hb, _NT, preferred_element_type=F32).astype(BF16)
    iota_n = lax.broadcasted_iota(jnp.int32, (N_KEYS, tR), 0).astype(F32)

    iota_r = lax.broadcasted_iota(jnp.int32, (PEER_TOPK, tR), 0).astype(F32)

    def topk_round(exact_ties, r, carry):
        s, rank, tops = carry
        rf = lax.convert_element_type(r, F32)
        m = jnp.max(s, axis=0, keepdims=True)
        sel = s == m
        if exact_ties:
            sel = iota_n == jnp.min(jnp.where(sel, iota_n, float(N_KEYS)), axis=0, keepdims=True)
        return jnp.where(sel, NEG_INF, s), jnp.where(sel, rf, rank), jnp.where(iota_r == rf, m, tops)

    def topk(s, exact_ties):
        init = (s, jnp.full((N_KEYS, tR), float(PEER_TOPK), F32), jnp.zeros((PEER_TOPK, tR), F32))
        _, rank, tops = lax.fori_loop(0, PEER_TOPK, functools.partial(topk_round, exact_ties), init)
        return rank, tops

    for hp in range(2 * PEER_HEADS):
        qh = q_scr[hp * PEER_HALF:(hp + 1) * PEER_HALF, :]
        s = jnp.dot(keys_ref[hp], qh, preferred_element_type=F32)
        s_scr[hp] = s
        rank, tops = topk(s, False)
        n_ranked = jnp.sum(jnp.where(rank < float(PEER_TOPK), 1.0, 0.0), axis=0, keepdims=True)
        rank, tops = lax.cond(jnp.max(n_ranked) > float(PEER_TOPK),
                              lambda: topk(s_scr[hp], True), lambda: (rank, tops))
        r_scr[hp] = rank
        base = ((hp % 2) * PEER_HEADS + hp // 2) * PEER_TOPK
        for c in range(tR // 128):
            a_scr[c, base:base + PEER_TOPK, :] = tops[:, c * 128:(c + 1) * 128]

    def a_rows(start, size, stride):
        return jnp.concatenate([a_scr[c, pl.ds(start, size, stride=stride), :] for c in range(tR // 128)], axis=1)

    a1 = [a_rows(i, PEER_HEADS, PEER_TOPK) for i in range(PEER_TOPK)]
    a2 = [a_rows(PEER_HEADS * PEER_TOPK + j, PEER_HEADS, PEER_TOPK) for j in range(PEER_TOPK)]
    vals = [a1[i] + a2[j] for (i, j) in _PAIRS]
    flat = [float(i * PEER_TOPK + j) for (i, j) in _PAIRS]
    picked = [jnp.zeros((PEER_HEADS, tR), F32) for _ in _PAIRS]
    for r in range(PEER_TOPK):
        m = functools.reduce(jnp.maximum, vals)
        idx = functools.reduce(jnp.minimum, [jnp.where(v == m, f, 1e9) for v, f in zip(vals, flat)])
        for k in range(len(_PAIRS)):
            sel = idx == flat[k]
            vals[k] = jnp.where(sel, NEG_INF, vals[k])
            picked[k] = jnp.where(sel, 1.0, picked[k])
    e1 = [jnp.exp(a1[i] - a1[0]) for i in range(PEER_TOPK)]
    e2 = [jnp.exp(a2[j] - a2[0]) for j in range(PEER_TOPK)]
    z = jnp.zeros((PEER_HEADS, tR), F32)
    for i in range(PEER_TOPK):
        row = [k for k, (pi, _) in enumerate(_PAIRS) if pi == i]
        c_scr[i] = functools.reduce(lambda a, b: a + b, [picked[k] for k in row])
        z = z + e1[i] * functools.reduce(lambda a, b: a + b, [picked[k] * e2[_PAIRS[k][1]] for k in row])
    z_scr[...] = 1.0 / z

    for h in range(PEER_HEADS):
        rank1 = r_scr[2 * h]
        rank2 = r_scr[2 * h + 1]
        cnt = jnp.zeros((N_KEYS, tR), F32)
        for r in range(PEER_TOPK):
            cnt = jnp.where(rank1 == float(r), c_scr[r, h:h + 1, :], cnt)
        cnt1_ref[h] = cnt
        top1 = a1[0][h:h + 1]
        top2 = a2[0][h:h + 1]
        e1d = jnp.exp(s_scr[2 * h] - top1) * z_scr[h:h + 1, :]
        e1n_ref[h] = jnp.where(rank1 < float(PEER_TOPK), e1d, 0.0)
        rk2_ref[h] = rank2
        e2_ref[h] = jnp.where(rank2 < float(PEER_TOPK), jnp.exp(s_scr[2 * h + 1] - top2), 0.0)


def _router(x, sh, sc, g, wqt, keys, tR=256):
    T, D = x.shape
    seg_len = T // sh.shape[0]
    seg = lambda i: ((i * tR) // seg_len, 0, 0)
    gshape = jax.ShapeDtypeStruct((PEER_HEADS, N_KEYS, T), F32)
    gspec = pl.BlockSpec((PEER_HEADS, N_KEYS, tR), lambda i: (0, 0, i))
    return pl.pallas_call(
        _router_kernel,
        grid=(T // tR,),
        in_specs=[pl.BlockSpec((tR, D), lambda i: (i, 0)),
                  pl.BlockSpec((1, 1, D), seg),
                  pl.BlockSpec((1, 1, D), seg),
                  pl.BlockSpec((1, D), lambda i: (0, 0)),
                  pl.BlockSpec(wqt.shape, lambda i: (0, 0)),
                  pl.BlockSpec(keys.shape, lambda i: (0, 0, 0))],
        out_specs=[pl.BlockSpec((tR, D), lambda i: (i, 0)), gspec, gspec, gspec, gspec],
        out_shape=[jax.ShapeDtypeStruct((T, D), BF16), gshape, gshape, gshape, gshape],
        scratch_shapes=[pltpu.VMEM((PEER_HEADS * PEER_KEY_DIM, tR), BF16),
                        pltpu.VMEM((2 * PEER_HEADS, N_KEYS, tR), F32),
                        pltpu.VMEM((2 * PEER_HEADS, N_KEYS, tR), F32),
                        pltpu.VMEM((tR // 128, 2 * PEER_HEADS * PEER_TOPK, 128), F32),
                        pltpu.VMEM((PEER_TOPK, PEER_HEADS, tR), F32),
                        pltpu.VMEM((PEER_HEADS, tR), F32)],
        compiler_params=pltpu.CompilerParams(dimension_semantics=("parallel",)),
        name="peer_router",
    )(x, sh, sc, g, wqt, keys)


def _experts_kernel(hb_ref, u_ref, vt_ref, cnt1_ref, e1n_ref, rk2_ref, e2_ref, x_ref, gf_ref,
                    o_ref, acc_ref, w0_ref, w1_ref):
    i = pl.program_id(1)
    n_tiles = pl.num_programs(1) - 1
    eT, tT = w0_ref.shape
    rows = eT // N_KEYS

    @pl.when(i == 0)
    def _():
        acc_ref[...] = jnp.zeros_like(acc_ref)
        w1_ref[...] = jnp.zeros_like(w1_ref)

    row0 = pl.multiple_of(jnp.minimum(i, n_tiles - 1) * rows, rows)

    def step(w_prev_ref, w_cur_ref):
        halves = [slice(0, tT // 2), slice(tT // 2, tT)]
        a_halves = [lax.dot_general(u_ref[...], hb_ref[hs, :], _NT, preferred_element_type=F32)
                    for hs in halves]
        for c in range(tT // 128):
            cols = slice(c * 128, (c + 1) * 128)
            half, cc = divmod(c, tT // 256)
            cnt = [cnt1_ref[h, pl.ds(row0, rows), cols] for h in range(PEER_HEADS)]
            e1 = [e1n_ref[h, pl.ds(row0, rows), cols] for h in range(PEER_HEADS)]
            for r in range(rows):
                g = jnp.zeros((N_KEYS, 128), F32)
                for h in range(PEER_HEADS):
                    g = g + jnp.where(rk2_ref[h, :, cols] < cnt[h][r:r + 1],
                                      e2_ref[h, :, cols] * e1[h][r:r + 1], 0.0)
                a = a_halves[half][r * N_KEYS:(r + 1) * N_KEYS, cc * 128:(cc + 1) * 128]
                w_cur_ref[r * N_KEYS:(r + 1) * N_KEYS, cols] = (g * jax.nn.gelu(a)).astype(BF16)
            if cc == tT // 256 - 1:
                hs = halves[half]
                acc_ref[:, hs] += jnp.dot(vt_ref[...], w_prev_ref[:, hs], preferred_element_type=F32)

    @pl.when(lax.rem(i, 2) == 0)
    def _():
        step(w1_ref, w0_ref)

    @pl.when(lax.rem(i, 2) == 1)
    def _():
        step(w0_ref, w1_ref)

    @pl.when(i == n_tiles)
    def _():
        o_ref[...] = x_ref[...] + gf_ref[0] * acc_ref[...].T


def _experts(hb, u_bf, vt_bf, cnt1, e1n, rk2, e2, x, gf, tT=512, eT=1024):
    T, D = x.shape
    seg_len = T // gf.shape[0]
    n_tiles = N_EXPERTS // eT
    gspec = pl.BlockSpec((PEER_HEADS, N_KEYS, tT), lambda j, i: (0, 0, j))
    return pl.pallas_call(
        _experts_kernel,
        grid=(T // tT, n_tiles + 1),
        in_specs=[pl.BlockSpec((tT, D), lambda j, i: (j, 0)),
                  pl.BlockSpec((eT, D), lambda j, i: (jnp.minimum(i, n_tiles - 1), 0)),
                  pl.BlockSpec((D, eT), lambda j, i: (0, jnp.maximum(i - 1, 0))),
                  gspec, gspec, gspec, gspec,
                  pl.BlockSpec((tT, D), lambda j, i: (j, 0)),
                  pl.BlockSpec((1, 1, D), lambda j, i: ((j * tT) // seg_len, 0, 0))],
        out_specs=pl.BlockSpec((tT, D), lambda j, i: (j, 0)),
        out_shape=jax.ShapeDtypeStruct((T, D), F32),
        scratch_shapes=[pltpu.VMEM((D, tT), F32), pltpu.VMEM((eT, tT), BF16), pltpu.VMEM((eT, tT), BF16)],
        compiler_params=pltpu.CompilerParams(dimension_semantics=("parallel", "arbitrary"),
                                             vmem_limit_bytes=48 * 1024 * 1024),
        name="peer_experts",
    )(hb, u_bf, vt_bf, cnt1, e1n, rk2, e2, x, gf)


def _peer_block(x, sh_f, sc_f, g_f, P):
    B, L, D = x.shape
    xf = x.reshape(B * L, D)
    hb, cnt1, e1n, rk2, e2 = _router(xf, sh_f, sc_f, P['norm_ffn_g'].reshape(1, D), P['peer_wqt'], P['peer_keys_bf'])
    out = _experts(hb, P['peer_u_bf'], P['peer_vt_bf'], cnt1, e1n, rk2, e2, xf, g_f)
    return out.reshape(B, L, D)


def _layer(x, mod, P, rope, k_ext, v_ext):
    B, L, D = x.shape
    sh_m, sc_m, g_m, sh_f, sc_f, g_f = jnp.split(mod, N_MOD, axis=-1)
    h = _rmsnorm(x, P['norm_mix_g']) * (1.0 + sc_m) + sh_m
    proj = _mm(h.reshape(B * L, D), P['w_in']).reshape(B, L, -1)
    q = _rmsnorm(proj[..., :Q_END].reshape(B, L, N_HEADS, HEAD_DIM), P['q_norm_g'])
    k = _rmsnorm(proj[..., Q_END:K_END].reshape(B, L, N_KV_HEADS, HEAD_DIM), P['k_norm_g'])
    v = proj[..., K_END:V_END].reshape(B, L, N_KV_HEADS, HEAD_DIM)
    if rope is None:
        attn = _attention(q, k, v)
    else:
        cos, sin = rope
        q_r = _apply_rope(q, cos, sin)
        k_r = _apply_rope(k, cos, sin)
        k_all = jnp.concatenate([k_ext, k_r], axis=1)
        v_all = jnp.concatenate([v_ext, v], axis=1)
        attn = _attention(q_r, k_all, v_all)
    hy = _hyena(proj[..., V_END:HY_END], P)
    sg = _chunk_gmlp(proj[..., HY_END:], P['sg_norm_g'], P['sg_w'], P['sg_b'])
    cat = jnp.concatenate([attn, hy, sg], axis=-1)
    mixed = _mm(cat.reshape(B * L, -1), P['w_out']).reshape(B, L, D)
    x = x + g_m * mixed
    x = _peer_block(x, sh_f, sc_f, g_f, P)
    return x, k, v


def kernel(x_prompt, x_sample, cache_k, cache_v, c, c_ctx, norm_mix_g, norm_ffn_g, ada_w, ada_b,
           w_in, w_out, q_norm_g, k_norm_g, hy_conv_w, hy_conv_b, hy_fw1, hy_fb1, hy_freq1,
           hy_fw2, hy_fb2, hy_freq2, hy_fw3, hy_decay, hy_skip, sg_norm_g, sg_w, sg_b,
           peer_wq, peer_keys, peer_u, peer_v):
    rope = _rope_tables(x_sample.shape[1])
    y_p = x_prompt
    y_s = x_sample
    ks_ctx = []
    vs_ctx = []
    for l in range(DEPTH):
        P = {
            'norm_mix_g': norm_mix_g[l], 'norm_ffn_g': norm_ffn_g[l],
            'w_in': w_in[l], 'w_out': w_out[l],
            'q_norm_g': q_norm_g[l], 'k_norm_g': k_norm_g[l],
            'hy_conv_w': hy_conv_w[l], 'hy_conv_b': hy_conv_b[l],
            'hy_fw1': hy_fw1[l], 'hy_fb1': hy_fb1[l], 'hy_freq1': hy_freq1[l],
            'hy_fw2': hy_fw2[l], 'hy_fb2': hy_fb2[l], 'hy_freq2': hy_freq2[l],
            'hy_fw3': hy_fw3[l], 'hy_decay': hy_decay[l], 'hy_skip': hy_skip[l],
            'sg_norm_g': sg_norm_g[l], 'sg_w': sg_w[l], 'sg_b': sg_b[l],
            'peer_wqt': peer_wq[l].T.astype(BF16),
            'peer_keys_bf': peer_keys[l].reshape(2 * PEER_HEADS, N_KEYS, PEER_HALF).astype(BF16),
            'peer_u_bf': peer_u[l].astype(BF16), 'peer_vt_bf': peer_v[l].T.astype(BF16),
        }
        mod_ctx = (jax.nn.silu(c_ctx)[None, :] @ ada_w[l] + ada_b[l])[:, None, :]
        mod_lat = (jax.nn.silu(c) @ ada_w[l] + ada_b[l])[:, None, :]
        y_p, k_l, v_l = _layer(y_p, mod_ctx, P, None, None, None)
        ks_ctx.append(k_l)
        vs_ctx.append(v_l)
        y_s, _, _ = _layer(y_s, mod_lat, P, rope, cache_k[:, l], cache_v[:, l])
    new_cache_k = jnp.stack(ks_ctx, axis=1)
    new_cache_v = jnp.stack(vs_ctx, axis=1)
    return (y_p, y_s, new_cache_k, new_cache_v)
```

```python
import functools

import jax
import jax.numpy as jnp
from jax import lax
from jax.experimental import pallas as pl
from jax.experimental.pallas import tpu as pltpu

D_MODEL = 1024
DEPTH = 2
GRID_W = 64
N_MOD = 6
EPS = 1e-6
N_HEADS = 8
N_KV_HEADS = 2
HEAD_DIM = 64
ATTN_W = N_HEADS * HEAD_DIM
KV_W = N_KV_HEADS * HEAD_DIM
Q_BLOCK = 128
ROPE_THETA = 10000.0
HY_CH = 256
HY_BANDS = 16
SG_W = 256
SG_GROUPS = 4
SG_GROUP_W = SG_W // SG_GROUPS
CHUNK = 128
Q_END = ATTN_W
K_END = Q_END + KV_W
V_END = K_END + KV_W
HY_END = V_END + 3 * HY_CH
PEER_HEADS = 8
N_KEYS = 128
PEER_KEY_DIM = 128
PEER_HALF = PEER_KEY_DIM // 2
PEER_TOPK = 16
TOKEN_BLOCK = 128

F32 = jnp.float32
BF16 = jnp.bfloat16


def _mm_kernel(a_ref, b_ref, o_ref):
    o_ref[...] = jnp.dot(a_ref[...].astype(BF16), b_ref[...].astype(BF16),
                         preferred_element_type=F32)


def _mm(a, b, tm=512, tn=512):
    M, K = a.shape
    _, N = b.shape
    tm = min(tm, M)
    tn = min(tn, N)
    return pl.pallas_call(
        _mm_kernel,
        grid=(M // tm, N // tn),
        in_specs=[pl.BlockSpec((tm, K), lambda i, j: (i, 0)),
                  pl.BlockSpec((K, tn), lambda i, j: (0, j))],
        out_specs=pl.BlockSpec((tm, tn), lambda i, j: (i, j)),
        out_shape=jax.ShapeDtypeStruct((M, N), F32),
    )(a, b)


def _rmsnorm(x, g):
    xf = x.astype(F32)
    y = xf * lax.rsqrt(jnp.mean(xf * xf, axis=-1, keepdims=True) + EPS)
    return (y * g.astype(F32)).astype(x.dtype)


def _rope_tables(L):
    rows = L // GRID_W
    row = jnp.broadcast_to(jnp.arange(rows, dtype=F32)[:, None], (rows, GRID_W)).reshape(L)
    col = jnp.broadcast_to(jnp.arange(GRID_W, dtype=F32)[None, :], (rows, GRID_W)).reshape(L)
    n_axis = HEAD_DIM // 4
    inv = ROPE_THETA ** (-jnp.arange(n_axis, dtype=F32) / n_axis)
    ang = jnp.concatenate([row[:, None] * inv, col[:, None] * inv], axis=-1)
    return jnp.cos(ang), jnp.sin(ang)


def _apply_rope(x, cos, sin):
    B, L, H, _ = x.shape
    xp = x.astype(F32).reshape(B, L, H, HEAD_DIM // 2, 2)
    a, b = xp[..., 0], xp[..., 1]
    c = cos[None, :, None, :]
    s = sin[None, :, None, :]
    out = jnp.stack([a * c - b * s, a * s + b * c], axis=-1).reshape(B, L, H, HEAD_DIM)
    return out.astype(x.dtype)


GROUP = N_HEADS // N_KV_HEADS
_NT = (((1,), (1,)), ((), ()))


def _attn_kernel(q_ref, k_ref, v_ref, o_ref):
    k = k_ref[0, 0]
    v = v_ref[0, 0]
    outs = []
    for g in range(GROUP):
        qg = q_ref[0, :, g * HEAD_DIM:(g + 1) * HEAD_DIM]
        s = lax.dot_general(qg, k, _NT, preferred_element_type=F32) * (HEAD_DIM ** -0.5)
        p = jnp.exp(s - jnp.max(s, axis=-1, keepdims=True))
        l = jnp.sum(p, axis=-1, keepdims=True)
        outs.append(jnp.dot(p.astype(BF16), v, preferred_element_type=F32) / l)
    o_ref[0] = jnp.concatenate(outs, axis=-1)


def _attention(q, k, v, tq=256):
    B, Lq, H, Dh = q.shape
    Lk = k.shape[1]
    qf = q.reshape(B, Lq, H * Dh).astype(BF16)
    kt = k.transpose(0, 2, 1, 3).astype(BF16)
    vt = v.transpose(0, 2, 1, 3).astype(BF16)
    return pl.pallas_call(
        _attn_kernel,
        grid=(B, N_KV_HEADS, Lq // tq),
        in_specs=[pl.BlockSpec((1, tq, GROUP * Dh), lambda b, kh, i: (b, i, kh)),
                  pl.BlockSpec((1, 1, Lk, Dh), lambda b, kh, i: (b, kh, 0, 0)),
                  pl.BlockSpec((1, 1, Lk, Dh), lambda b, kh, i: (b, kh, 0, 0))],
        out_specs=pl.BlockSpec((1, tq, GROUP * Dh), lambda b, kh, i: (b, i, kh)),
        out_shape=jax.ShapeDtypeStruct((B, Lq, H * Dh), F32),
        compiler_params=pltpu.CompilerParams(dimension_semantics=("parallel", "parallel", "parallel"),
                                             vmem_limit_bytes=48 * 1024 * 1024),
        name="attention",
    )(qf, kt, vt)


def _short_conv(z, w, b):
    zp = jnp.pad(z, ((0, 0), (1, 1), (0, 0)))
    return zp[:, :-2] * w[0] + zp[:, 1:-1] * w[1] + zp[:, 2:] * w[2] + b


def _hyena_filter_fft(L, fw1, fb1, fr1, fw2, fb2, fr2, fw3, decay):
    t = jnp.arange(L, dtype=F32)
    tn = t / (L - 1)
    w = 2.0 * jnp.pi * t / L
    f = jnp.linspace(1e-4, HY_BANDS - 1, HY_BANDS, dtype=F32)
    zf = w[:, None] * f[None, :]
    feats = jnp.concatenate([tn[:, None], jnp.cos(zf), -jnp.sin(zf)], axis=-1)
    h = jnp.sin(fr1 * (feats @ fw1 + fb1))
    h = jnp.sin(fr2 * (h @ fw2 + fb2))
    h = (h @ fw3).reshape(L, 2, HY_CH)
    h = h * jnp.exp(-tn[:, None, None] * jnp.abs(decay))
    fwd, bwd = h[:, 0], h[:, 1]
    full = jnp.concatenate([fwd, jnp.zeros((1, HY_CH), F32), bwd[:0:-1]], axis=0)
    return jnp.fft.rfft(full, axis=0)


def _hyena(z, P):
    B, L, _ = z.shape
    z = _short_conv(z, P['hy_conv_w'], P['hy_conv_b'])
    x0, x1, v = jnp.split(z, 3, axis=-1)
    u = (x1 * v).astype(F32)
    Hf = _hyena_filter_fft(L, P['hy_fw1'], P['hy_fb1'], P['hy_freq1'], P['hy_fw2'],
                           P['hy_fb2'], P['hy_freq2'], P['hy_fw3'], P['hy_decay'])
    U = jnp.fft.rfft(u, n=2 * L, axis=1)
    conv = jnp.fft.irfft(U * Hf[None], n=2 * L, axis=1)[:, :L]
    return x0 * (conv + u * P['hy_skip'])


def _chunk_gmlp(z, g, ws, bs):
    B, L, _ = z.shape
    u, v = jnp.split(jax.nn.gelu(z), 2, axis=-1)
    v = _rmsnorm(v, g).reshape(B, L // CHUNK, CHUNK, SG_GROUPS, SG_GROUP_W)
    v = jnp.einsum('hpq,bnqhc->bnphc', ws, v) + bs.T[None, None, :, :, None]
    return u * v.reshape(B, L, SG_W)


N_EXPERTS = N_KEYS * N_KEYS
NEG_INF = float('-inf')
_PAIRS = [(i, j) for i in range(PEER_TOPK) for j in range(PEER_TOPK) if (i + 1) * (j + 1) <= PEER_TOPK]


def _router_kernel(x_ref, sh_ref, sc_ref, g_ref, wqt_ref, keys_ref,
                   hb_ref, cnt1_ref, e1n_ref, rk2_ref, e2_ref,
                   q_scr, s_scr, r_scr, a_scr, c_scr, z_scr):
    tR = x_ref.shape[0]
    x = x_ref[...]
    y = x * lax.rsqrt(jnp.mean(x * x, axis=-1, keepdims=True) + EPS) * g_ref[...]
    hb = (y * (1.0 + sc_ref[0]) + sh_ref[0]).astype(BF16)
    hb_ref[...] = hb
    q_scr[...] = lax.dot_general(wqt_ref[...], hb, _NT, preferred_element_type=F32).astype(BF16)
    iota_n = lax.broadcasted_iota(jnp.int32, (N_KEYS, tR), 0).astype(F32)

    iota_r = lax.broadcasted_iota(jnp.int32, (PEER_TOPK, tR), 0).astype(F32)

    def topk_round(exact_ties, r, carry):
        s, rank, tops = carry
        rf = lax.convert_element_type(r, F32)
        m = jnp.max(s, axis=0, keepdims=True)
        sel = s == m
        if exact_ties:
            sel = iota_n == jnp.min(jnp.where(sel, iota_n, float(N_KEYS)), axis=0, keepdims=True)
        return jnp.where(sel, NEG_INF, s), jnp.where(sel, rf, rank), jnp.where(iota_r == rf, m, tops)

    def topk(s, exact_ties):
        init = (s, jnp.full((N_KEYS, tR), float(PEER_TOPK), F32), jnp.zeros((PEER_TOPK, tR), F32))
        _, rank, tops = lax.fori_loop(0, PEER_TOPK, functools.partial(topk_round, exact_ties), init)
        return rank, tops

    for hp in range(2 * PEER_HEADS):
        qh = q_scr[hp * PEER_HALF:(hp + 1) * PEER_HALF, :]
        s = jnp.dot(keys_ref[hp], qh, preferred_element_type=F32)
        s_scr[hp] = s
        rank, tops = topk(s, False)
        n_ranked = jnp.sum(jnp.where(rank < float(PEER_TOPK), 1.0, 0.0), axis=0, keepdims=True)
        rank, tops = lax.cond(jnp.max(n_ranked) > float(PEER_TOPK),
                              lambda: topk(s_scr[hp], True), lambda: (rank, tops))
        r_scr[hp] = rank
        base = ((hp % 2) * PEER_HEADS + hp // 2) * PEER_TOPK
        for c in range(tR // 128):
            a_scr[c, base:base + PEER_TOPK, :] = tops[:, c * 128:(c + 1) * 128]

    def a_rows(start, size, stride):
        return jnp.concatenate([a_scr[c, pl.ds(start, size, stride=stride), :] for c in range(tR // 128)], axis=1)

    a1 = [a_rows(i, PEER_HEADS, PEER_TOPK) for i in range(PEER_TOPK)]
    a2 = [a_rows(PEER_HEADS * PEER_TOPK + j, PEER_HEADS, PEER_TOPK) for j in range(PEER_TOPK)]
    vals = [a1[i] + a2[j] for (i, j) in _PAIRS]
    flat = [float(i * PEER_TOPK + j) for (i, j) in _PAIRS]
    picked = [jnp.zeros((PEER_HEADS, tR), F32) for _ in _PAIRS]
    for r in range(PEER_TOPK):
        m = functools.reduce(jnp.maximum, vals)
        idx = functools.reduce(jnp.minimum, [jnp.where(v == m, f, 1e9) for v, f in zip(vals, flat)])
        for k in range(len(_PAIRS)):
            sel = idx == flat[k]
            vals[k] = jnp.where(sel, NEG_INF, vals[k])
            picked[k] = jnp.where(sel, 1.0, picked[k])
    e1 = [jnp.exp(a1[i] - a1[0]) for i in range(PEER_TOPK)]
    e2 = [jnp.exp(a2[j] - a2[0]) for j in range(PEER_TOPK)]
    z = jnp.zeros((PEER_HEADS, tR), F32)
    for i in range(PEER_TOPK):
        row = [k for k, (pi, _) in enumerate(_PAIRS) if pi == i]
        c_scr[i] = functools.reduce(lambda a, b: a + b, [picked[k] for k in row])
        z = z + e1[i] * functools.reduce(lambda a, b: a + b, [picked[k] * e2[_PAIRS[k][1]] for k in row])
    z_scr[...] = 1.0 / z

    for h in range(PEER_HEADS):
        rank1 = r_scr[2 * h]
        rank2 = r_scr[2 * h + 1]
        cnt = jnp.zeros((N_KEYS, tR), F32)
        for r in range(PEER_TOPK):
            cnt = jnp.where(rank1 == float(r), c_scr[r, h:h + 1, :], cnt)
        cnt1_ref[h] = cnt
        top1 = a1[0][h:h + 1]
        top2 = a2[0][h:h + 1]
        e1d = jnp.exp(s_scr[2 * h] - top1) * z_scr[h:h + 1, :]
        e1n_ref[h] = jnp.where(rank1 < float(PEER_TOPK), e1d, 0.0)
        rk2_ref[h] = rank2
        e2_ref[h] = jnp.where(rank2 < float(PEER_TOPK), jnp.exp(s_scr[2 * h + 1] - top2), 0.0)


def _router(x, sh, sc, g, wqt, keys, tR=256):
    T, D = x.shape
    seg_len = T // sh.shape[0]
    seg = lambda i: ((i * tR) // seg_len, 0, 0)
    gshape = jax.ShapeDtypeStruct((PEER_HEADS, N_KEYS, T), F32)
    gspec = pl.BlockSpec((PEER_HEADS, N_KEYS, tR), lambda i: (0, 0, i))
    return pl.pallas_call(
        _router_kernel,
        grid=(T // tR,),
        in_specs=[pl.BlockSpec((tR, D), lambda i: (i, 0)),
                  pl.BlockSpec((1, 1, D), seg),
                  pl.BlockSpec((1, 1, D), seg),
                  pl.BlockSpec((1, D), lambda i: (0, 0)),
                  pl.BlockSpec(wqt.shape, lambda i: (0, 0)),
                  pl.BlockSpec(keys.shape, lambda i: (0, 0, 0))],
        out_specs=[pl.BlockSpec((tR, D), lambda i: (i, 0)), gspec, gspec, gspec, gspec],
        out_shape=[jax.ShapeDtypeStruct((T, D), BF16), gshape, gshape, gshape, gshape],
        scratch_shapes=[pltpu.VMEM((PEER_HEADS * PEER_KEY_DIM, tR), BF16),
                        pltpu.VMEM((2 * PEER_HEADS, N_KEYS, tR), F32),
                        pltpu.VMEM((2 * PEER_HEADS, N_KEYS, tR), F32),
                        pltpu.VMEM((tR // 128, 2 * PEER_HEADS * PEER_TOPK, 128), F32),
                        pltpu.VMEM((PEER_TOPK, PEER_HEADS, tR), F32),
                        pltpu.VMEM((PEER_HEADS, tR), F32)],
        compiler_params=pltpu.CompilerParams(dimension_semantics=("parallel",)),
        name="peer_router",
    )(x, sh, sc, g, wqt, keys)


def _experts_kernel(hb_ref, u_ref, vt_ref, cnt1_ref, e1n_ref, rk2_ref, e2_ref, x_ref, gf_ref,
                    o_ref, acc_ref, w_ref):
    i = pl.program_id(1)
    eT, tT = w_ref.shape
    rows = eT // N_KEYS

    @pl.when(i == 0)
    def _():
        acc_ref[...] = jnp.zeros_like(acc_ref)

    a_t = lax.dot_general(u_ref[...], hb_ref[...], _NT, preferred_element_type=F32)
    row0 = pl.multiple_of(i * rows, rows)
    for c in range(tT // 128):
        cols = slice(c * 128, (c + 1) * 128)
        cnt = [cnt1_ref[h, pl.ds(row0, rows), cols] for h in range(PEER_HEADS)]
        e1 = [e1n_ref[h, pl.ds(row0, rows), cols] for h in range(PEER_HEADS)]
        for r in range(rows):
            g = None
            for h in range(PEER_HEADS):
                gh = jnp.where(rk2_ref[h, :, cols] < cnt[h][r:r + 1], e2_ref[h, :, cols] * e1[h][r:r + 1], 0.0)
                g = gh if g is None else g + gh
            a = a_t[r * N_KEYS:(r + 1) * N_KEYS, cols]
            w_ref[r * N_KEYS:(r + 1) * N_KEYS, cols] = (g * jax.nn.gelu(a)).astype(BF16)
    acc_ref[...] += jnp.dot(vt_ref[...], w_ref[...], preferred_element_type=F32)

    @pl.when(i == pl.num_programs(1) - 1)
    def _():
        o_ref[...] = x_ref[...] + gf_ref[0] * acc_ref[...].T


def _experts(hb, u_bf, vt_bf, cnt1, e1n, rk2, e2, x, gf, tT=512, eT=1024):
    T, D = x.shape
    seg_len = T // gf.shape[0]
    gspec = pl.BlockSpec((PEER_HEADS, N_KEYS, tT), lambda j, i: (0, 0, j))
    return pl.pallas_call(
        _experts_kernel,
        grid=(T // tT, N_EXPERTS // eT),
        in_specs=[pl.BlockSpec((tT, D), lambda j, i: (j, 0)),
                  pl.BlockSpec((eT, D), lambda j, i: (i, 0)),
                  pl.BlockSpec((D, eT), lambda j, i: (0, i)),
                  gspec, gspec, gspec, gspec,
                  pl.BlockSpec((tT, D), lambda j, i: (j, 0)),
                  pl.BlockSpec((1, 1, D), lambda j, i: ((j * tT) // seg_len, 0, 0))],
        out_specs=pl.BlockSpec((tT, D), lambda j, i: (j, 0)),
        out_shape=jax.ShapeDtypeStruct((T, D), F32),
        scratch_shapes=[pltpu.VMEM((D, tT), F32), pltpu.VMEM((eT, tT), BF16)],
        compiler_params=pltpu.CompilerParams(dimension_semantics=("parallel", "arbitrary"),
                                             vmem_limit_bytes=48 * 1024 * 1024),
        name="peer_experts",
    )(hb, u_bf, vt_bf, cnt1, e1n, rk2, e2, x, gf)


def _peer_block(x, sh_f, sc_f, g_f, P):
    B, L, D = x.shape
    xf = x.reshape(B * L, D)
    hb, cnt1, e1n, rk2, e2 = _router(xf, sh_f, sc_f, P['norm_ffn_g'].reshape(1, D), P['peer_wqt'], P['peer_keys_bf'])
    out = _experts(hb, P['peer_u_bf'], P['peer_vt_bf'], cnt1, e1n, rk2, e2, xf, g_f)
    return out.reshape(B, L, D)


def _layer(x, mod, P, rope, k_ext, v_ext):
    B, L, D = x.shape
    sh_m, sc_m, g_m, sh_f, sc_f, g_f = jnp.split(mod, N_MOD, axis=-1)
    h = _rmsnorm(x, P['norm_mix_g']) * (1.0 + sc_m) + sh_m
    proj = _mm(h.reshape(B * L, D), P['w_in']).reshape(B, L, -1)
    q = _rmsnorm(proj[..., :Q_END].reshape(B, L, N_HEADS, HEAD_DIM), P['q_norm_g'])
    k = _rmsnorm(proj[..., Q_END:K_END].reshape(B, L, N_KV_HEADS, HEAD_DIM), P['k_norm_g'])
    v = proj[..., K_END:V_END].reshape(B, L, N_KV_HEADS, HEAD_DIM)
    if rope is None:
        attn = _attention(q, k, v)
    else:
        cos, sin = rope
        q_r = _apply_rope(q, cos, sin)
        k_r = _apply_rope(k, cos, sin)
        k_all = jnp.concatenate([k_ext, k_r], axis=1)
        v_all = jnp.concatenate([v_ext, v], axis=1)
        attn = _attention(q_r, k_all, v_all)
    hy = _hyena(proj[..., V_END:HY_END], P)
    sg = _chunk_gmlp(proj[..., HY_END:], P['sg_norm_g'], P['sg_w'], P['sg_b'])
    cat = jnp.concatenate([attn, hy, sg], axis=-1)
    mixed = _mm(cat.reshape(B * L, -1), P['w_out']).reshape(B, L, D)
    x = x + g_m * mixed
    x = _peer_block(x, sh_f, sc_f, g_f, P)
    return x, k, v


def kernel(x_prompt, x_sample, cache_k, cache_v, c, c_ctx, norm_mix_g, norm_ffn_g, ada_w, ada_b,
           w_in, w_out, q_norm_g, k_norm_g, hy_conv_w, hy_conv_b, hy_fw1, hy_fb1, hy_freq1,
           hy_fw2, hy_fb2, hy_freq2, hy_fw3, hy_decay, hy_skip, sg_norm_g, sg_w, sg_b,
           peer_wq, peer_keys, peer_u, peer_v):
    rope = _rope_tables(x_sample.shape[1])
    y_p = x_prompt
    y_s = x_sample
    ks_ctx = []
    vs_ctx = []
    for l in range(DEPTH):
        P = {
            'norm_mix_g': norm_mix_g[l], 'norm_ffn_g': norm_ffn_g[l],
            'w_in': w_in[l], 'w_out': w_out[l],
            'q_norm_g': q_norm_g[l], 'k_norm_g': k_norm_g[l],
            'hy_conv_w': hy_conv_w[l], 'hy_conv_b': hy_conv_b[l],
            'hy_fw1': hy_fw1[l], 'hy_fb1': hy_fb1[l], 'hy_freq1': hy_freq1[l],
            'hy_fw2': hy_fw2[l], 'hy_fb2': hy_fb2[l], 'hy_freq2': hy_freq2[l],
            'hy_fw3': hy_fw3[l], 'hy_decay': hy_decay[l], 'hy_skip': hy_skip[l],
            'sg_norm_g': sg_norm_g[l], 'sg_w': sg_w[l], 'sg_b': sg_b[l],
            'peer_wqt': peer_wq[l].T.astype(BF16),
            'peer_keys_bf': peer_keys[l].reshape(2 * PEER_HEADS, N_KEYS, PEER_HALF).astype(BF16),
            'peer_u_bf': peer_u[l].astype(BF16), 'peer_vt_bf': peer_v[l].T.astype(BF16),
        }
        mod_ctx = (jax.nn.silu(c_ctx)[None, :] @ ada_w[l] + ada_b[l])[:, None, :]
        mod_lat = (jax.nn.silu(c) @ ada_w[l] + ada_b[l])[:, None, :]
        y_p, k_l, v_l = _layer(y_p, mod_ctx, P, None, None, None)
        ks_ctx.append(k_l)
        vs_ctx.append(v_l)
        y_s, _, _ = _layer(y_s, mod_lat, P, rope, cache_k[:, l], cache_v[:, l])
    new_cache_k = jnp.stack(ks_ctx, axis=1)
    new_cache_v = jnp.stack(vs_ctx, axis=1)
    return (y_p, y_s, new_cache_k, new_cache_v)
```

```python
import functools

import jax
import jax.numpy as jnp
from jax import lax
from jax.experimental import pallas as pl
from jax.experimental.pallas import tpu as pltpu

D_MODEL = 1024
DEPTH = 2
GRID_W = 64
N_MOD = 6
EPS = 1e-6
N_HEADS = 8
N_KV_HEADS = 2
HEAD_DIM = 64
ATTN_W = N_HEADS * HEAD_DIM
KV_W = N_KV_HEADS * HEAD_DIM
Q_BLOCK = 128
ROPE_THETA = 10000.0
HY_CH = 256
HY_BANDS = 16
SG_W = 256
SG_GROUPS = 4
SG_GROUP_W = SG_W // SG_GROUPS
CHUNK = 128
Q_END = ATTN_W
K_END = Q_END + KV_W
V_END = K_END + KV_W
HY_END = V_END + 3 * HY_CH
PEER_HEADS = 8
N_KEYS = 128
PEER_KEY_DIM = 128
PEER_HALF = PEER_KEY_DIM // 2
PEER_TOPK = 16
TOKEN_BLOCK = 128

F32 = jnp.float32
BF16 = jnp.bfloat16


def _mm_kernel(a_ref, b_ref, o_ref):
    o_ref[...] = jnp.dot(a_ref[...].astype(BF16), b_ref[...].astype(BF16),
                         preferred_element_type=F32)


def _mm(a, b, tm=512, tn=512):
    M, K = a.shape
    _, N = b.shape
    tm = min(tm, M)
    tn = min(tn, N)
    return pl.pallas_call(
        _mm_kernel,
        grid=(M // tm, N // tn),
        in_specs=[pl.BlockSpec((tm, K), lambda i, j: (i, 0)),
                  pl.BlockSpec((K, tn), lambda i, j: (0, j))],
        out_specs=pl.BlockSpec((tm, tn), lambda i, j: (i, j)),
        out_shape=jax.ShapeDtypeStruct((M, N), F32),
    )(a, b)


def _rmsnorm(x, g):
    xf = x.astype(F32)
    y = xf * lax.rsqrt(jnp.mean(xf * xf, axis=-1, keepdims=True) + EPS)
    return (y * g.astype(F32)).astype(x.dtype)


def _rope_tables(L):
    rows = L // GRID_W
    row = jnp.broadcast_to(jnp.arange(rows, dtype=F32)[:, None], (rows, GRID_W)).reshape(L)
    col = jnp.broadcast_to(jnp.arange(GRID_W, dtype=F32)[None, :], (rows, GRID_W)).reshape(L)
    n_axis = HEAD_DIM // 4
    inv = ROPE_THETA ** (-jnp.arange(n_axis, dtype=F32) / n_axis)
    ang = jnp.concatenate([row[:, None] * inv, col[:, None] * inv], axis=-1)
    return jnp.cos(ang), jnp.sin(ang)


def _apply_rope(x, cos, sin):
    B, L, H, _ = x.shape
    xp = x.astype(F32).reshape(B, L, H, HEAD_DIM // 2, 2)
    a, b = xp[..., 0], xp[..., 1]
    c = cos[None, :, None, :]
    s = sin[None, :, None, :]
    out = jnp.stack([a * c - b * s, a * s + b * c], axis=-1).reshape(B, L, H, HEAD_DIM)
    return out.astype(x.dtype)


GROUP = N_HEADS // N_KV_HEADS
_NT = (((1,), (1,)), ((), ()))


def _attn_kernel(q_ref, k_ref, v_ref, o_ref):
    k = k_ref[0, 0]
    v = v_ref[0, 0]
    outs = []
    for g in range(GROUP):
        qg = q_ref[0, :, g * HEAD_DIM:(g + 1) * HEAD_DIM]
        s = lax.dot_general(qg, k, _NT, preferred_element_type=F32) * (HEAD_DIM ** -0.5)
        p = jnp.exp(s - jnp.max(s, axis=-1, keepdims=True))
        l = jnp.sum(p, axis=-1, keepdims=True)
        outs.append(jnp.dot(p.astype(BF16), v, preferred_element_type=F32) / l)
    o_ref[0] = jnp.concatenate(outs, axis=-1)


def _attention(q, k, v, tq=256):
    B, Lq, H, Dh = q.shape
    Lk = k.shape[1]
    qf = q.reshape(B, Lq, H * Dh).astype(BF16)
    kt = k.transpose(0, 2, 1, 3).astype(BF16)
    vt = v.transpose(0, 2, 1, 3).astype(BF16)
    return pl.pallas_call(
        _attn_kernel,
        grid=(B, N_KV_HEADS, Lq // tq),
        in_specs=[pl.BlockSpec((1, tq, GROUP * Dh), lambda b, kh, i: (b, i, kh)),
                  pl.BlockSpec((1, 1, Lk, Dh), lambda b, kh, i: (b, kh, 0, 0)),
                  pl.BlockSpec((1, 1, Lk, Dh), lambda b, kh, i: (b, kh, 0, 0))],
        out_specs=pl.BlockSpec((1, tq, GROUP * Dh), lambda b, kh, i: (b, i, kh)),
        out_shape=jax.ShapeDtypeStruct((B, Lq, H * Dh), F32),
        compiler_params=pltpu.CompilerParams(dimension_semantics=("parallel", "parallel", "parallel"),
                                             vmem_limit_bytes=48 * 1024 * 1024),
        name="attention",
    )(qf, kt, vt)


def _short_conv(z, w, b):
    zp = jnp.pad(z, ((0, 0), (1, 1), (0, 0)))
    return zp[:, :-2] * w[0] + zp[:, 1:-1] * w[1] + zp[:, 2:] * w[2] + b


def _hyena_filter_fft(L, fw1, fb1, fr1, fw2, fb2, fr2, fw3, decay):
    t = jnp.arange(L, dtype=F32)
    tn = t / (L - 1)
    w = 2.0 * jnp.pi * t / L
    f = jnp.linspace(1e-4, HY_BANDS - 1, HY_BANDS, dtype=F32)
    zf = w[:, None] * f[None, :]
    feats = jnp.concatenate([tn[:, None], jnp.cos(zf), -jnp.sin(zf)], axis=-1)
    h = jnp.sin(fr1 * (feats @ fw1 + fb1))
    h = jnp.sin(fr2 * (h @ fw2 + fb2))
    h = (h @ fw3).reshape(L, 2, HY_CH)
    h = h * jnp.exp(-tn[:, None, None] * jnp.abs(decay))
    fwd, bwd = h[:, 0], h[:, 1]
    full = jnp.concatenate([fwd, jnp.zeros((1, HY_CH), F32), bwd[:0:-1]], axis=0)
    return jnp.fft.rfft(full, axis=0)


def _hyena(z, P):
    B, L, _ = z.shape
    z = _short_conv(z, P['hy_conv_w'], P['hy_conv_b'])
    x0, x1, v = jnp.split(z, 3, axis=-1)
    u = (x1 * v).astype(F32)
    Hf = _hyena_filter_fft(L, P['hy_fw1'], P['hy_fb1'], P['hy_freq1'], P['hy_fw2'],
                           P['hy_fb2'], P['hy_freq2'], P['hy_fw3'], P['hy_decay'])
    U = jnp.fft.rfft(u, n=2 * L, axis=1)
    conv = jnp.fft.irfft(U * Hf[None], n=2 * L, axis=1)[:, :L]
    return x0 * (conv + u * P['hy_skip'])


def _chunk_gmlp(z, g, ws, bs):
    B, L, _ = z.shape
    u, v = jnp.split(jax.nn.gelu(z), 2, axis=-1)
    v = _rmsnorm(v, g).reshape(B, L // CHUNK, CHUNK, SG_GROUPS, SG_GROUP_W)
    v = jnp.einsum('hpq,bnqhc->bnphc', ws, v) + bs.T[None, None, :, :, None]
    return u * v.reshape(B, L, SG_W)


N_EXPERTS = N_KEYS * N_KEYS
NEG_INF = float('-inf')
_PAIRS = [(i, j) for i in range(PEER_TOPK) for j in range(PEER_TOPK) if (i + 1) * (j + 1) <= PEER_TOPK]


def _router_kernel(x_ref, sh_ref, sc_ref, g_ref, wqt_ref, keys_ref,
                   hb_ref, cnt1_ref, e1n_ref, rk2_ref, e2_ref,
                   q_scr, s_scr, r_scr, a_scr, c_scr, z_scr):
    tR = x_ref.shape[0]
    x = x_ref[...]
    y = x * lax.rsqrt(jnp.mean(x * x, axis=-1, keepdims=True) + EPS) * g_ref[...]
    hb = (y * (1.0 + sc_ref[0]) + sh_ref[0]).astype(BF16)
    hb_ref[...] = hb
    q_scr[...] = lax.dot_general(wqt_ref[...], hb, _NT, preferred_element_type=F32).astype(BF16)
    iota_n = lax.broadcasted_iota(jnp.int32, (N_KEYS, tR), 0).astype(F32)

    iota_r = lax.broadcasted_iota(jnp.int32, (PEER_TOPK, tR), 0).astype(F32)

    def topk_round(exact_ties, r, carry):
        s, rank, tops = carry
        rf = lax.convert_element_type(r, F32)
        m = jnp.max(s, axis=0, keepdims=True)
        sel = s == m
        if exact_ties:
            sel = iota_n == jnp.min(jnp.where(sel, iota_n, float(N_KEYS)), axis=0, keepdims=True)
        return jnp.where(sel, NEG_INF, s), jnp.where(sel, rf, rank), jnp.where(iota_r == rf, m, tops)

    def topk(s, exact_ties):
        init = (s, jnp.full((N_KEYS, tR), float(PEER_TOPK), F32), jnp.zeros((PEER_TOPK, tR), F32))
        _, rank, tops = lax.fori_loop(0, PEER_TOPK, functools.partial(topk_round, exact_ties), init)
        return rank, tops

    for hp in range(2 * PEER_HEADS):
        qh = q_scr[hp * PEER_HALF:(hp + 1) * PEER_HALF, :]
        s = jnp.dot(keys_ref[hp], qh, preferred_element_type=F32)
        s_scr[hp] = s
        rank, tops = topk(s, False)
        n_ranked = jnp.sum(jnp.where(rank < float(PEER_TOPK), 1.0, 0.0), axis=0, keepdims=True)
        rank, tops = lax.cond(jnp.max(n_ranked) > float(PEER_TOPK),
                              lambda: topk(s_scr[hp], True), lambda: (rank, tops))
        r_scr[hp] = rank
        base = ((hp % 2) * PEER_HEADS + hp // 2) * PEER_TOPK
        for c in range(tR // 128):
            a_scr[c, base:base + PEER_TOPK, :] = tops[:, c * 128:(c + 1) * 128]

    def a_rows(start, size, stride):
        return jnp.concatenate([a_scr[c, pl.ds(start, size, stride=stride), :] for c in range(tR // 128)], axis=1)

    a1 = [a_rows(i, PEER_HEADS, PEER_TOPK) for i in range(PEER_TOPK)]
    a2 = [a_rows(PEER_HEADS * PEER_TOPK + j, PEER_HEADS, PEER_TOPK) for j in range(PEER_TOPK)]
    vals = [a1[i] + a2[j] for (i, j) in _PAIRS]
    flat = [float(i * PEER_TOPK + j) for (i, j) in _PAIRS]
    picked = [jnp.zeros((PEER_HEADS, tR), F32) for _ in _PAIRS]
    for r in range(PEER_TOPK):
        m = functools.reduce(jnp.maximum, vals)
        idx = functools.reduce(jnp.minimum, [jnp.where(v == m, f, 1e9) for v, f in zip(vals, flat)])
        for k in range(len(_PAIRS)):
            sel = idx == flat[k]
            vals[k] = jnp.where(sel, NEG_INF, vals[k])
            picked[k] = jnp.where(sel, 1.0, picked[k])
    e1 = [jnp.exp(a1[i] - a1[0]) for i in range(PEER_TOPK)]
    e2 = [jnp.exp(a2[j] - a2[0]) for j in range(PEER_TOPK)]
    z = jnp.zeros((PEER_HEADS, tR), F32)
    for i in range(PEER_TOPK):
        row = [k for k, (pi, _) in enumerate(_PAIRS) if pi == i]
        c_scr[i] = functools.reduce(lambda a, b: a + b, [picked[k] for k in row])
        z = z + e1[i] * functools.reduce(lambda a, b: a + b, [picked[k] * e2[_PAIRS[k][1]] for k in row])
    z_scr[...] = 1.0 / z

    for h in range(PEER_HEADS):
        rank1 = r_scr[2 * h]
        rank2 = r_scr[2 * h + 1]
        cnt = jnp.zeros((N_KEYS, tR), F32)
        for r in range(PEER_TOPK):
            cnt = jnp.where(rank1 == float(r), c_scr[r, h:h + 1, :], cnt)
        cnt1_ref[h] = cnt
        top1 = a1[0][h:h + 1]
        top2 = a2[0][h:h + 1]
        e1d = jnp.exp(s_scr[2 * h] - top1) * z_scr[h:h + 1, :]
        e1n_ref[h] = jnp.where(rank1 < float(PEER_TOPK), e1d, 0.0)
        rk2_ref[h] = rank2.astype(BF16)
        e2_ref[h] = jnp.where(rank2 < float(PEER_TOPK), jnp.exp(s_scr[2 * h + 1] - top2), 0.0).astype(BF16)


def _router(x, sh, sc, g, wqt, keys, tR=256):
    T, D = x.shape
    seg_len = T // sh.shape[0]
    seg = lambda i: ((i * tR) // seg_len, 0, 0)
    gshape = jax.ShapeDtypeStruct((PEER_HEADS, N_KEYS, T), F32)
    gspec = pl.BlockSpec((PEER_HEADS, N_KEYS, tR), lambda i: (0, 0, i))
    return pl.pallas_call(
        _router_kernel,
        grid=(T // tR,),
        in_specs=[pl.BlockSpec((tR, D), lambda i: (i, 0)),
                  pl.BlockSpec((1, 1, D), seg),
                  pl.BlockSpec((1, 1, D), seg),
                  pl.BlockSpec((1, D), lambda i: (0, 0)),
                  pl.BlockSpec(wqt.shape, lambda i: (0, 0)),
                  pl.BlockSpec(keys.shape, lambda i: (0, 0, 0))],
        out_specs=[pl.BlockSpec((tR, D), lambda i: (i, 0)), gspec, gspec, gspec, gspec],
        out_shape=[jax.ShapeDtypeStruct((T, D), BF16), gshape, gshape,
                   jax.ShapeDtypeStruct(gshape.shape, BF16), jax.ShapeDtypeStruct(gshape.shape, BF16)],
        scratch_shapes=[pltpu.VMEM((PEER_HEADS * PEER_KEY_DIM, tR), BF16),
                        pltpu.VMEM((2 * PEER_HEADS, N_KEYS, tR), F32),
                        pltpu.VMEM((2 * PEER_HEADS, N_KEYS, tR), F32),
                        pltpu.VMEM((tR // 128, 2 * PEER_HEADS * PEER_TOPK, 128), F32),
                        pltpu.VMEM((PEER_TOPK, PEER_HEADS, tR), F32),
                        pltpu.VMEM((PEER_HEADS, tR), F32)],
        compiler_params=pltpu.CompilerParams(dimension_semantics=("parallel",)),
        name="peer_router",
    )(x, sh, sc, g, wqt, keys)


def _experts_kernel(hb_ref, u_ref, vt_ref, cnt1_ref, e1n_ref, rk2_ref, e2_ref, x_ref, gf_ref,
                    o_ref, acc_ref, w_ref):
    i = pl.program_id(1)
    eT, tT = w_ref.shape
    rows = eT // N_KEYS

    @pl.when(i == 0)
    def _():
        acc_ref[...] = jnp.zeros_like(acc_ref)

    a_t = lax.dot_general(u_ref[...], hb_ref[...], _NT, preferred_element_type=F32)
    row0 = pl.multiple_of(i * rows, rows)
    for c in range(tT // 128):
        cols = slice(c * 128, (c + 1) * 128)
        cnt = [cnt1_ref[h, pl.ds(row0, rows), cols].astype(BF16) for h in range(PEER_HEADS)]
        e1 = [e1n_ref[h, pl.ds(row0, rows), cols].astype(BF16) for h in range(PEER_HEADS)]
        for r in range(rows):
            g = None
            for h in range(PEER_HEADS):
                gh = jnp.where(rk2_ref[h, :, cols] < cnt[h][r:r + 1], e2_ref[h, :, cols] * e1[h][r:r + 1],
                               jnp.zeros((), BF16))
                g = gh if g is None else g + gh
            a = a_t[r * N_KEYS:(r + 1) * N_KEYS, cols]
            w_ref[r * N_KEYS:(r + 1) * N_KEYS, cols] = (g.astype(F32) * jax.nn.gelu(a)).astype(BF16)
    acc_ref[...] += jnp.dot(vt_ref[...], w_ref[...], preferred_element_type=F32)

    @pl.when(i == pl.num_programs(1) - 1)
    def _():
        o_ref[...] = x_ref[...] + gf_ref[0] * acc_ref[...].T


def _experts(hb, u_bf, vt_bf, cnt1, e1n, rk2, e2, x, gf, tT=512, eT=1024):
    T, D = x.shape
    seg_len = T // gf.shape[0]
    gspec = pl.BlockSpec((PEER_HEADS, N_KEYS, tT), lambda j, i: (0, 0, j))
    return pl.pallas_call(
        _experts_kernel,
        grid=(T // tT, N_EXPERTS // eT),
        in_specs=[pl.BlockSpec((tT, D), lambda j, i: (j, 0)),
                  pl.BlockSpec((eT, D), lambda j, i: (i, 0)),
                  pl.BlockSpec((D, eT), lambda j, i: (0, i)),
                  gspec, gspec, gspec, gspec,
                  pl.BlockSpec((tT, D), lambda j, i: (j, 0)),
                  pl.BlockSpec((1, 1, D), lambda j, i: ((j * tT) // seg_len, 0, 0))],
        out_specs=pl.BlockSpec((tT, D), lambda j, i: (j, 0)),
        out_shape=jax.ShapeDtypeStruct((T, D), F32),
        scratch_shapes=[pltpu.VMEM((D, tT), F32), pltpu.VMEM((eT, tT), BF16)],
        compiler_params=pltpu.CompilerParams(dimension_semantics=("parallel", "arbitrary"),
                                             vmem_limit_bytes=48 * 1024 * 1024),
        name="peer_experts",
    )(hb, u_bf, vt_bf, cnt1, e1n, rk2, e2, x, gf)


def _peer_block(x, sh_f, sc_f, g_f, P):
    B, L, D = x.shape
    xf = x.reshape(B * L, D)
    hb, cnt1, e1n, rk2, e2 = _router(xf, sh_f, sc_f, P['norm_ffn_g'].reshape(1, D), P['peer_wqt'], P['peer_keys_bf'])
    out = _experts(hb, P['peer_u_bf'], P['peer_vt_bf'], cnt1, e1n, rk2, e2, xf, g_f)
    return out.reshape(B, L, D)


def _layer(x, mod, P, rope, k_ext, v_ext):
    B, L, D = x.shape
    sh_m, sc_m, g_m, sh_f, sc_f, g_f = jnp.split(mod, N_MOD, axis=-1)
    h = _rmsnorm(x, P['norm_mix_g']) * (1.0 + sc_m) + sh_m
    proj = _mm(h.reshape(B * L, D), P['w_in']).reshape(B, L, -1)
    q = _rmsnorm(proj[..., :Q_END].reshape(B, L, N_HEADS, HEAD_DIM), P['q_norm_g'])
    k = _rmsnorm(proj[..., Q_END:K_END].reshape(B, L, N_KV_HEADS, HEAD_DIM), P['k_norm_g'])
    v = proj[..., K_END:V_END].reshape(B, L, N_KV_HEADS, HEAD_DIM)
    if rope is None:
        attn = _attention(q, k, v)
    else:
        cos, sin = rope
        q_r = _apply_rope(q, cos, sin)
        k_r = _apply_rope(k, cos, sin)
        k_all = jnp.concatenate([k_ext, k_r], axis=1)
        v_all = jnp.concatenate([v_ext, v], axis=1)
        attn = _attention(q_r, k_all, v_all)
    hy = _hyena(proj[..., V_END:HY_END], P)
    sg = _chunk_gmlp(proj[..., HY_END:], P['sg_norm_g'], P['sg_w'], P['sg_b'])
    cat = jnp.concatenate([attn, hy, sg], axis=-1)
    mixed = _mm(cat.reshape(B * L, -1), P['w_out']).reshape(B, L, D)
    x = x + g_m * mixed
    x = _peer_block(x, sh_f, sc_f, g_f, P)
    return x, k, v


def kernel(x_prompt, x_sample, cache_k, cache_v, c, c_ctx, norm_mix_g, norm_ffn_g, ada_w, ada_b,
           w_in, w_out, q_norm_g, k_norm_g, hy_conv_w, hy_conv_b, hy_fw1, hy_fb1, hy_freq1,
           hy_fw2, hy_fb2, hy_freq2, hy_fw3, hy_decay, hy_skip, sg_norm_g, sg_w, sg_b,
           peer_wq, peer_keys, peer_u, peer_v):
    rope = _rope_tables(x_sample.shape[1])
    y_p = x_prompt
    y_s = x_sample
    ks_ctx = []
    vs_ctx = []
    for l in range(DEPTH):
        P = {
            'norm_mix_g': norm_mix_g[l], 'norm_ffn_g': norm_ffn_g[l],
            'w_in': w_in[l], 'w_out': w_out[l],
            'q_norm_g': q_norm_g[l], 'k_norm_g': k_norm_g[l],
            'hy_conv_w': hy_conv_w[l], 'hy_conv_b': hy_conv_b[l],
            'hy_fw1': hy_fw1[l], 'hy_fb1': hy_fb1[l], 'hy_freq1': hy_freq1[l],
            'hy_fw2': hy_fw2[l], 'hy_fb2': hy_fb2[l], 'hy_freq2': hy_freq2[l],
            'hy_fw3': hy_fw3[l], 'hy_decay': hy_decay[l], 'hy_skip': hy_skip[l],
            'sg_norm_g': sg_norm_g[l], 'sg_w': sg_w[l], 'sg_b': sg_b[l],
            'peer_wqt': peer_wq[l].T.astype(BF16),
            'peer_keys_bf': peer_keys[l].reshape(2 * PEER_HEADS, N_KEYS, PEER_HALF).astype(BF16),
            'peer_u_bf': peer_u[l].astype(BF16), 'peer_vt_bf': peer_v[l].T.astype(BF16),
        }
        mod_ctx = (jax.nn.silu(c_ctx)[None, :] @ ada_w[l] + ada_b[l])[:, None, :]
        mod_lat = (jax.nn.silu(c) @ ada_w[l] + ada_b[l])[:, None, :]
        y_p, k_l, v_l = _layer(y_p, mod_ctx, P, None, None, None)
        ks_ctx.append(k_l)
        vs_ctx.append(v_l)
        y_s, _, _ = _layer(y_s, mod_lat, P, rope, cache_k[:, l], cache_v[:, l])
    new_cache_k = jnp.stack(ks_ctx, axis=1)
    new_cache_v = jnp.stack(vs_ctx, axis=1)
    return (y_p, y_s, new_cache_k, new_cache_v)
```

```python
import functools

import jax
import jax.numpy as jnp
from jax import lax
from jax.experimental import pallas as pl
from jax.experimental.pallas import tpu as pltpu

D_MODEL = 1024
DEPTH = 2
GRID_W = 64
N_MOD = 6
EPS = 1e-6
N_HEADS = 8
N_KV_HEADS = 2
HEAD_DIM = 64
ATTN_W = N_HEADS * HEAD_DIM
KV_W = N_KV_HEADS * HEAD_DIM
Q_BLOCK = 128
ROPE_THETA = 10000.0
HY_CH = 256
HY_BANDS = 16
SG_W = 256
SG_GROUPS = 4
SG_GROUP_W = SG_W // SG_GROUPS
CHUNK = 128
Q_END = ATTN_W
K_END = Q_END + KV_W
V_END = K_END + KV_W
HY_END = V_END + 3 * HY_CH
PEER_HEADS = 8
N_KEYS = 128
PEER_KEY_DIM = 128
PEER_HALF = PEER_KEY_DIM // 2
PEER_TOPK = 16
TOKEN_BLOCK = 128

F32 = jnp.float32
BF16 = jnp.bfloat16


def _in_proj_kernel(x_ref, sh_ref, sc_ref, g_ref, w_ref, o_ref):
    x = x_ref[...]
    y = x * lax.rsqrt(jnp.mean(x * x, axis=-1, keepdims=True) + EPS) * g_ref[...]
    h = (y * (1.0 + sc_ref[0]) + sh_ref[0]).astype(BF16)
    o_ref[...] = jnp.dot(h, w_ref[...], preferred_element_type=F32)


def _in_proj(x, sh, sc, g, w_bf, tm=512):
    T, D = x.shape
    N = w_bf.shape[1]
    seg_len = T // sh.shape[0]
    seg = lambda i: ((i * tm) // seg_len, 0, 0)
    return pl.pallas_call(
        _in_proj_kernel,
        grid=(T // tm,),
        in_specs=[pl.BlockSpec((tm, D), lambda i: (i, 0)),
                  pl.BlockSpec((1, 1, D), seg),
                  pl.BlockSpec((1, 1, D), seg),
                  pl.BlockSpec((1, D), lambda i: (0, 0)),
                  pl.BlockSpec((D, N), lambda i: (0, 0))],
        out_specs=pl.BlockSpec((tm, N), lambda i: (i, 0)),
        out_shape=jax.ShapeDtypeStruct((T, N), F32),
        compiler_params=pltpu.CompilerParams(dimension_semantics=("parallel",),
                                             vmem_limit_bytes=40 * 1024 * 1024),
        name="in_proj",
    )(x, sh, sc, g, w_bf)


def _out_proj_kernel(a_ref, w_ref, x_ref, gm_ref, o_ref):
    mixed = jnp.dot(a_ref[...].astype(BF16), w_ref[...], preferred_element_type=F32)
    o_ref[...] = x_ref[...] + gm_ref[0] * mixed


def _out_proj(a, w_bf, x, gm, tm=512):
    T, K = a.shape
    D = w_bf.shape[1]
    seg_len = T // gm.shape[0]
    return pl.pallas_call(
        _out_proj_kernel,
        grid=(T // tm,),
        in_specs=[pl.BlockSpec((tm, K), lambda i: (i, 0)),
                  pl.BlockSpec((K, D), lambda i: (0, 0)),
                  pl.BlockSpec((tm, D), lambda i: (i, 0)),
                  pl.BlockSpec((1, 1, D), lambda i: ((i * tm) // seg_len, 0, 0))],
        out_specs=pl.BlockSpec((tm, D), lambda i: (i, 0)),
        out_shape=jax.ShapeDtypeStruct((T, D), F32),
        compiler_params=pltpu.CompilerParams(dimension_semantics=("parallel",),
                                             vmem_limit_bytes=40 * 1024 * 1024),
        name="out_proj",
    )(a, w_bf, x, gm)


def _rmsnorm(x, g):
    xf = x.astype(F32)
    y = xf * lax.rsqrt(jnp.mean(xf * xf, axis=-1, keepdims=True) + EPS)
    return (y * g.astype(F32)).astype(x.dtype)


def _rope_tables(L):
    rows = L // GRID_W
    row = jnp.broadcast_to(jnp.arange(rows, dtype=F32)[:, None], (rows, GRID_W)).reshape(L)
    col = jnp.broadcast_to(jnp.arange(GRID_W, dtype=F32)[None, :], (rows, GRID_W)).reshape(L)
    n_axis = HEAD_DIM // 4
    inv = ROPE_THETA ** (-jnp.arange(n_axis, dtype=F32) / n_axis)
    ang = jnp.concatenate([row[:, None] * inv, col[:, None] * inv], axis=-1)
    return jnp.cos(ang), jnp.sin(ang)


def _apply_rope(x, cos, sin):
    B, L, H, _ = x.shape
    xp = x.astype(F32).reshape(B, L, H, HEAD_DIM // 2, 2)
    a, b = xp[..., 0], xp[..., 1]
    c = cos[None, :, None, :]
    s = sin[None, :, None, :]
    out = jnp.stack([a * c - b * s, a * s + b * c], axis=-1).reshape(B, L, H, HEAD_DIM)
    return out.astype(x.dtype)


GROUP = N_HEADS // N_KV_HEADS
_NT = (((1,), (1,)), ((), ()))


def _attn_kernel(q_ref, k_ref, v_ref, o_ref):
    k = k_ref[0, 0]
    v = v_ref[0, 0]
    outs = []
    for g in range(GROUP):
        qg = q_ref[0, :, g * HEAD_DIM:(g + 1) * HEAD_DIM]
        s = lax.dot_general(qg, k, _NT, preferred_element_type=F32) * (HEAD_DIM ** -0.5)
        p = jnp.exp(s - jnp.max(s, axis=-1, keepdims=True))
        l = jnp.sum(p, axis=-1, keepdims=True)
        outs.append(jnp.dot(p.astype(BF16), v, preferred_element_type=F32) / l)
    o_ref[0] = jnp.concatenate(outs, axis=-1)


def _attention(q, k, v, tq=256):
    B, Lq, H, Dh = q.shape
    Lk = k.shape[1]
    qf = q.reshape(B, Lq, H * Dh).astype(BF16)
    kt = k.transpose(0, 2, 1, 3).astype(BF16)
    vt = v.transpose(0, 2, 1, 3).astype(BF16)
    return pl.pallas_call(
        _attn_kernel,
        grid=(B, N_KV_HEADS, Lq // tq),
        in_specs=[pl.BlockSpec((1, tq, GROUP * Dh), lambda b, kh, i: (b, i, kh)),
                  pl.BlockSpec((1, 1, Lk, Dh), lambda b, kh, i: (b, kh, 0, 0)),
                  pl.BlockSpec((1, 1, Lk, Dh), lambda b, kh, i: (b, kh, 0, 0))],
        out_specs=pl.BlockSpec((1, tq, GROUP * Dh), lambda b, kh, i: (b, i, kh)),
        out_shape=jax.ShapeDtypeStruct((B, Lq, H * Dh), F32),
        compiler_params=pltpu.CompilerParams(dimension_semantics=("parallel", "parallel", "parallel"),
                                             vmem_limit_bytes=48 * 1024 * 1024),
        name="attention",
    )(qf, kt, vt)


def _short_conv(z, w, b):
    zp = jnp.pad(z, ((0, 0), (1, 1), (0, 0)))
    return zp[:, :-2] * w[0] + zp[:, 1:-1] * w[1] + zp[:, 2:] * w[2] + b


def _hyena_filter_fft(L, fw1, fb1, fr1, fw2, fb2, fr2, fw3, decay):
    t = jnp.arange(L, dtype=F32)
    tn = t / (L - 1)
    w = 2.0 * jnp.pi * t / L
    f = jnp.linspace(1e-4, HY_BANDS - 1, HY_BANDS, dtype=F32)
    zf = w[:, None] * f[None, :]
    feats = jnp.concatenate([tn[:, None], jnp.cos(zf), -jnp.sin(zf)], axis=-1)
    h = jnp.sin(fr1 * (feats @ fw1 + fb1))
    h = jnp.sin(fr2 * (h @ fw2 + fb2))
    h = (h @ fw3).reshape(L, 2, HY_CH)
    h = h * jnp.exp(-tn[:, None, None] * jnp.abs(decay))
    fwd, bwd = h[:, 0], h[:, 1]
    full = jnp.concatenate([fwd, jnp.zeros((1, HY_CH), F32), bwd[:0:-1]], axis=0)
    return jnp.fft.rfft(full, axis=0)


def _hyena(z, P):
    B, L, _ = z.shape
    z = _short_conv(z, P['hy_conv_w'], P['hy_conv_b'])
    x0, x1, v = jnp.split(z, 3, axis=-1)
    u = (x1 * v).astype(F32)
    Hf = _hyena_filter_fft(L, P['hy_fw1'], P['hy_fb1'], P['hy_freq1'], P['hy_fw2'],
                           P['hy_fb2'], P['hy_freq2'], P['hy_fw3'], P['hy_decay'])
    U = jnp.fft.rfft(u, n=2 * L, axis=1)
    conv = jnp.fft.irfft(U * Hf[None], n=2 * L, axis=1)[:, :L]
    return x0 * (conv + u * P['hy_skip'])


def _chunk_gmlp(z, g, ws, bs):
    B, L, _ = z.shape
    u, v = jnp.split(jax.nn.gelu(z), 2, axis=-1)
    v = _rmsnorm(v, g).reshape(B, L // CHUNK, CHUNK, SG_GROUPS, SG_GROUP_W)
    v = jnp.einsum('hpq,bnqhc->bnphc', ws, v) + bs.T[None, None, :, :, None]
    return u * v.reshape(B, L, SG_W)


N_EXPERTS = N_KEYS * N_KEYS
NEG_INF = float('-inf')
_PAIRS = [(i, j) for i in range(PEER_TOPK) for j in range(PEER_TOPK) if (i + 1) * (j + 1) <= PEER_TOPK]


def _router_kernel(x_ref, sh_ref, sc_ref, g_ref, wqt_ref, keys_ref,
                   hb_ref, cnt1_ref, e1n_ref, rk2_ref, e2_ref,
                   q_scr, s_scr, r_scr, a_scr, c_scr, z_scr):
    tR = x_ref.shape[0]
    x = x_ref[...]
    y = x * lax.rsqrt(jnp.mean(x * x, axis=-1, keepdims=True) + EPS) * g_ref[...]
    hb = (y * (1.0 + sc_ref[0]) + sh_ref[0]).astype(BF16)
    hb_ref[...] = hb
    q_scr[...] = lax.dot_general(wqt_ref[...], hb, _NT, preferred_element_type=F32).astype(BF16)
    iota_n = lax.broadcasted_iota(jnp.int32, (N_KEYS, tR), 0).astype(F32)

    iota_r = lax.broadcasted_iota(jnp.int32, (PEER_TOPK, tR), 0).astype(F32)

    def topk_round(exact_ties, r, carry):
        s, rank, tops = carry
        rf = lax.convert_element_type(r, F32)
        m = jnp.max(s, axis=0, keepdims=True)
        sel = s == m
        if exact_ties:
            sel = iota_n == jnp.min(jnp.where(sel, iota_n, float(N_KEYS)), axis=0, keepdims=True)
        return jnp.where(sel, NEG_INF, s), jnp.where(sel, rf, rank), jnp.where(iota_r == rf, m, tops)

    def topk(s, exact_ties):
        init = (s, jnp.full((N_KEYS, tR), float(PEER_TOPK), F32), jnp.zeros((PEER_TOPK, tR), F32))
        _, rank, tops = lax.fori_loop(0, PEER_TOPK, functools.partial(topk_round, exact_ties), init)
        return rank, tops

    for hp in range(2 * PEER_HEADS):
        qh = q_scr[hp * PEER_HALF:(hp + 1) * PEER_HALF, :]
        s = jnp.dot(keys_ref[hp], qh, preferred_element_type=F32)
        s_scr[hp] = s
        rank, tops = topk(s, False)
        n_ranked = jnp.sum(jnp.where(rank < float(PEER_TOPK), 1.0, 0.0), axis=0, keepdims=True)
        rank, tops = lax.cond(jnp.max(n_ranked) > float(PEER_TOPK),
                              lambda: topk(s_scr[hp], True), lambda: (rank, tops))
        r_scr[hp] = rank
        base = ((hp % 2) * PEER_HEADS + hp // 2) * PEER_TOPK
        for c in range(tR // 128):
            a_scr[c, base:base + PEER_TOPK, :] = tops[:, c * 128:(c + 1) * 128]

    def a_rows(start, size, stride):
        return jnp.concatenate([a_scr[c, pl.ds(start, size, stride=stride), :] for c in range(tR // 128)], axis=1)

    a1 = [a_rows(i, PEER_HEADS, PEER_TOPK) for i in range(PEER_TOPK)]
    a2 = [a_rows(PEER_HEADS * PEER_TOPK + j, PEER_HEADS, PEER_TOPK) for j in range(PEER_TOPK)]
    vals = [a1[i] + a2[j] for (i, j) in _PAIRS]
    flat = [float(i * PEER_TOPK + j) for (i, j) in _PAIRS]
    picked = [jnp.zeros((PEER_HEADS, tR), F32) for _ in _PAIRS]
    for r in range(PEER_TOPK):
        m = functools.reduce(jnp.maximum, vals)
        idx = functools.reduce(jnp.minimum, [jnp.where(v == m, f, 1e9) for v, f in zip(vals, flat)])
        for k in range(len(_PAIRS)):
            sel = idx == flat[k]
            vals[k] = jnp.where(sel, NEG_INF, vals[k])
            picked[k] = jnp.where(sel, 1.0, picked[k])
    e1 = [jnp.exp(a1[i] - a1[0]) for i in range(PEER_TOPK)]
    e2 = [jnp.exp(a2[j] - a2[0]) for j in range(PEER_TOPK)]
    z = jnp.zeros((PEER_HEADS, tR), F32)
    for i in range(PEER_TOPK):
        row = [k for k, (pi, _) in enumerate(_PAIRS) if pi == i]
        c_scr[i] = functools.reduce(lambda a, b: a + b, [picked[k] for k in row])
        z = z + e1[i] * functools.reduce(lambda a, b: a + b, [picked[k] * e2[_PAIRS[k][1]] for k in row])
    z_scr[...] = 1.0 / z

    for h in range(PEER_HEADS):
        rank1 = r_scr[2 * h]
        rank2 = r_scr[2 * h + 1]
        cnt = jnp.zeros((N_KEYS, tR), F32)
        for r in range(PEER_TOPK):
            cnt = jnp.where(rank1 == float(r), c_scr[r, h:h + 1, :], cnt)
        cnt1_ref[h] = cnt
        top1 = a1[0][h:h + 1]
        top2 = a2[0][h:h + 1]
        e1d = jnp.exp(s_scr[2 * h] - top1) * z_scr[h:h + 1, :]
        e1n_ref[h] = jnp.where(rank1 < float(PEER_TOPK), e1d, 0.0)
        rk2_ref[h] = rank2
        e2_ref[h] = jnp.where(rank2 < float(PEER_TOPK), jnp.exp(s_scr[2 * h + 1] - top2), 0.0)


def _router(x, sh, sc, g, wqt, keys, tR=256):
    T, D = x.shape
    seg_len = T // sh.shape[0]
    seg = lambda i: ((i * tR) // seg_len, 0, 0)
    gshape = jax.ShapeDtypeStruct((PEER_HEADS, N_KEYS, T), F32)
    gspec = pl.BlockSpec((PEER_HEADS, N_KEYS, tR), lambda i: (0, 0, i))
    return pl.pallas_call(
        _router_kernel,
        grid=(T // tR,),
        in_specs=[pl.BlockSpec((tR, D), lambda i: (i, 0)),
                  pl.BlockSpec((1, 1, D), seg),
                  pl.BlockSpec((1, 1, D), seg),
                  pl.BlockSpec((1, D), lambda i: (0, 0)),
                  pl.BlockSpec(wqt.shape, lambda i: (0, 0)),
                  pl.BlockSpec(keys.shape, lambda i: (0, 0, 0))],
        out_specs=[pl.BlockSpec((tR, D), lambda i: (i, 0)), gspec, gspec, gspec, gspec],
        out_shape=[jax.ShapeDtypeStruct((T, D), BF16), gshape, gshape, gshape, gshape],
        scratch_shapes=[pltpu.VMEM((PEER_HEADS * PEER_KEY_DIM, tR), BF16),
                        pltpu.VMEM((2 * PEER_HEADS, N_KEYS, tR), F32),
                        pltpu.VMEM((2 * PEER_HEADS, N_KEYS, tR), F32),
                        pltpu.VMEM((tR // 128, 2 * PEER_HEADS * PEER_TOPK, 128), F32),
                        pltpu.VMEM((PEER_TOPK, PEER_HEADS, tR), F32),
                        pltpu.VMEM((PEER_HEADS, tR), F32)],
        compiler_params=pltpu.CompilerParams(dimension_semantics=("parallel",)),
        name="peer_router",
    )(x, sh, sc, g, wqt, keys)


def _experts_kernel(hb_ref, u_ref, vt_ref, cnt1_ref, e1n_ref, rk2_ref, e2_ref, x_ref, gf_ref,
                    o_ref, acc_ref, w_ref):
    i = pl.program_id(1)
    eT, tT = w_ref.shape
    rows = eT // N_KEYS

    @pl.when(i == 0)
    def _():
        acc_ref[...] = jnp.zeros_like(acc_ref)

    a_t = lax.dot_general(u_ref[...], hb_ref[...], _NT, preferred_element_type=F32)
    row0 = pl.multiple_of(i * rows, rows)
    for c in range(tT // 128):
        cols = slice(c * 128, (c + 1) * 128)
        cnt = [cnt1_ref[h, pl.ds(row0, rows), cols] for h in range(PEER_HEADS)]
        e1 = [e1n_ref[h, pl.ds(row0, rows), cols] for h in range(PEER_HEADS)]
        for r in range(rows):
            g = None
            for h in range(PEER_HEADS):
                gh = jnp.where(rk2_ref[h, :, cols] < cnt[h][r:r + 1], e2_ref[h, :, cols] * e1[h][r:r + 1], 0.0)
                g = gh if g is None else g + gh
            a = a_t[r * N_KEYS:(r + 1) * N_KEYS, cols]
            w_ref[r * N_KEYS:(r + 1) * N_KEYS, cols] = (g * jax.nn.gelu(a)).astype(BF16)
    acc_ref[...] += jnp.dot(vt_ref[...], w_ref[...], preferred_element_type=F32)

    @pl.when(i == pl.num_programs(1) - 1)
    def _():
        o_ref[...] = x_ref[...] + gf_ref[0] * acc_ref[...].T


def _experts(hb, u_bf, vt_bf, cnt1, e1n, rk2, e2, x, gf, tT=512, eT=1024):
    T, D = x.shape
    seg_len = T // gf.shape[0]
    gspec = pl.BlockSpec((PEER_HEADS, N_KEYS, tT), lambda j, i: (0, 0, j))
    return pl.pallas_call(
        _experts_kernel,
        grid=(T // tT, N_EXPERTS // eT),
        in_specs=[pl.BlockSpec((tT, D), lambda j, i: (j, 0)),
                  pl.BlockSpec((eT, D), lambda j, i: (i, 0)),
                  pl.BlockSpec((D, eT), lambda j, i: (0, i)),
                  gspec, gspec, gspec, gspec,
                  pl.BlockSpec((tT, D), lambda j, i: (j, 0)),
                  pl.BlockSpec((1, 1, D), lambda j, i: ((j * tT) // seg_len, 0, 0))],
        out_specs=pl.BlockSpec((tT, D), lambda j, i: (j, 0)),
        out_shape=jax.ShapeDtypeStruct((T, D), F32),
        scratch_shapes=[pltpu.VMEM((D, tT), F32), pltpu.VMEM((eT, tT), BF16)],
        compiler_params=pltpu.CompilerParams(dimension_semantics=("parallel", "arbitrary"),
                                             vmem_limit_bytes=48 * 1024 * 1024),
        name="peer_experts",
    )(hb, u_bf, vt_bf, cnt1, e1n, rk2, e2, x, gf)


def _peer_block(x, sh_f, sc_f, g_f, P):
    B, L, D = x.shape
    xf = x.reshape(B * L, D)
    hb, cnt1, e1n, rk2, e2 = _router(xf, sh_f, sc_f, P['norm_ffn_g'].reshape(1, D), P['peer_wqt'], P['peer_keys_bf'])
    out = _experts(hb, P['peer_u_bf'], P['peer_vt_bf'], cnt1, e1n, rk2, e2, xf, g_f)
    return out.reshape(B, L, D)


def _layer(x, mod, P, rope, k_ext, v_ext):
    B, L, D = x.shape
    sh_m, sc_m, g_m, sh_f, sc_f, g_f = jnp.split(mod, N_MOD, axis=-1)
    proj = _in_proj(x.reshape(B * L, D), sh_m, sc_m, P['norm_mix_g'].reshape(1, D), P['w_in']).reshape(B, L, -1)
    q = _rmsnorm(proj[..., :Q_END].reshape(B, L, N_HEADS, HEAD_DIM), P['q_norm_g'])
    k = _rmsnorm(proj[..., Q_END:K_END].reshape(B, L, N_KV_HEADS, HEAD_DIM), P['k_norm_g'])
    v = proj[..., K_END:V_END].reshape(B, L, N_KV_HEADS, HEAD_DIM)
    if rope is None:
        attn = _attention(q, k, v)
    else:
        cos, sin = rope
        q_r = _apply_rope(q, cos, sin)
        k_r = _apply_rope(k, cos, sin)
        k_all = jnp.concatenate([k_ext, k_r], axis=1)
        v_all = jnp.concatenate([v_ext, v], axis=1)
        attn = _attention(q_r, k_all, v_all)
    hy = _hyena(proj[..., V_END:HY_END], P)
    sg = _chunk_gmlp(proj[..., HY_END:], P['sg_norm_g'], P['sg_w'], P['sg_b'])
    cat = jnp.concatenate([attn, hy, sg], axis=-1)
    x = _out_proj(cat.reshape(B * L, -1), P['w_out'], x.reshape(B * L, D), g_m).reshape(B, L, D)
    x = _peer_block(x, sh_f, sc_f, g_f, P)
    return x, k, v


def kernel(x_prompt, x_sample, cache_k, cache_v, c, c_ctx, norm_mix_g, norm_ffn_g, ada_w, ada_b,
           w_in, w_out, q_norm_g, k_norm_g, hy_conv_w, hy_conv_b, hy_fw1, hy_fb1, hy_freq1,
           hy_fw2, hy_fb2, hy_freq2, hy_fw3, hy_decay, hy_skip, sg_norm_g, sg_w, sg_b,
           peer_wq, peer_keys, peer_u, peer_v):
    rope = _rope_tables(x_sample.shape[1])
    y_p = x_prompt
    y_s = x_sample
    ks_ctx = []
    vs_ctx = []
    for l in range(DEPTH):
        P = {
            'norm_mix_g': norm_mix_g[l], 'norm_ffn_g': norm_ffn_g[l],
            'w_in': w_in[l].astype(BF16), 'w_out': w_out[l].astype(BF16),
            'q_norm_g': q_norm_g[l], 'k_norm_g': k_norm_g[l],
            'hy_conv_w': hy_conv_w[l], 'hy_conv_b': hy_conv_b[l],
            'hy_fw1': hy_fw1[l], 'hy_fb1': hy_fb1[l], 'hy_freq1': hy_freq1[l],
            'hy_fw2': hy_fw2[l], 'hy_fb2': hy_fb2[l], 'hy_freq2': hy_freq2[l],
            'hy_fw3': hy_fw3[l], 'hy_decay': hy_decay[l], 'hy_skip': hy_skip[l],
            'sg_norm_g': sg_norm_g[l], 'sg_w': sg_w[l], 'sg_b': sg_b[l],
            'peer_wqt': peer_wq[l].T.astype(BF16),
            'peer_keys_bf': peer_keys[l].reshape(2 * PEER_HEADS, N_KEYS, PEER_HALF).astype(BF16),
            'peer_u_bf': peer_u[l].astype(BF16), 'peer_vt_bf': peer_v[l].T.astype(BF16),
        }
        mod_ctx = (jax.nn.silu(c_ctx)[None, :] @ ada_w[l] + ada_b[l])[:, None, :]
        mod_lat = (jax.nn.silu(c) @ ada_w[l] + ada_b[l])[:, None, :]
        y_p, k_l, v_l = _layer(y_p, mod_ctx, P, None, None, None)
        ks_ctx.append(k_l)
        vs_ctx.append(v_l)
        y_s, _, _ = _layer(y_s, mod_lat, P, rope, cache_k[:, l], cache_v[:, l])
    new_cache_k = jnp.stack(ks_ctx, axis=1)
    new_cache_v = jnp.stack(vs_ctx, axis=1)
    return (y_p, y_s, new_cache_k, new_cache_v)
```

```python
import functools

import jax
import jax.numpy as jnp
from jax import lax
from jax.experimental import pallas as pl
from jax.experimental.pallas import tpu as pltpu

D_MODEL = 1024
DEPTH = 2
GRID_W = 64
N_MOD = 6
EPS = 1e-6
N_HEADS = 8
N_KV_HEADS = 2
HEAD_DIM = 64
ATTN_W = N_HEADS * HEAD_DIM
KV_W = N_KV_HEADS * HEAD_DIM
Q_BLOCK = 128
ROPE_THETA = 10000.0
HY_CH = 256
HY_BANDS = 16
SG_W = 256
SG_GROUPS = 4
SG_GROUP_W = SG_W // SG_GROUPS
CHUNK = 128
Q_END = ATTN_W
K_END = Q_END + KV_W
V_END = K_END + KV_W
HY_END = V_END + 3 * HY_CH
PEER_HEADS = 8
N_KEYS = 128
PEER_KEY_DIM = 128
PEER_HALF = PEER_KEY_DIM // 2
PEER_TOPK = 16
TOKEN_BLOCK = 128

F32 = jnp.float32
BF16 = jnp.bfloat16


def _in_proj_kernel(x_ref, sh_ref, sc_ref, g_ref, w_ref, o_ref):
    x = x_ref[...]
    y = x * lax.rsqrt(jnp.mean(x * x, axis=-1, keepdims=True) + EPS) * g_ref[...]
    h = (y * (1.0 + sc_ref[0]) + sh_ref[0]).astype(BF16)
    o_ref[...] = jnp.dot(h, w_ref[...], preferred_element_type=F32)


def _in_proj(x, sh, sc, g, w_bf, tm=512):
    T, D = x.shape
    N = w_bf.shape[1]
    seg_len = T // sh.shape[0]
    seg = lambda i: ((i * tm) // seg_len, 0, 0)
    return pl.pallas_call(
        _in_proj_kernel,
        grid=(T // tm,),
        in_specs=[pl.BlockSpec((tm, D), lambda i: (i, 0)),
                  pl.BlockSpec((1, 1, D), seg),
                  pl.BlockSpec((1, 1, D), seg),
                  pl.BlockSpec((1, D), lambda i: (0, 0)),
                  pl.BlockSpec((D, N), lambda i: (0, 0))],
        out_specs=pl.BlockSpec((tm, N), lambda i: (i, 0)),
        out_shape=jax.ShapeDtypeStruct((T, N), F32),
        compiler_params=pltpu.CompilerParams(dimension_semantics=("parallel",),
                                             vmem_limit_bytes=40 * 1024 * 1024),
        name="in_proj",
    )(x, sh, sc, g, w_bf)


def _out_proj_kernel(a_ref, w_ref, x_ref, gm_ref, o_ref):
    mixed = jnp.dot(a_ref[...].astype(BF16), w_ref[...], preferred_element_type=F32)
    o_ref[...] = x_ref[...] + gm_ref[0] * mixed


def _out_proj(a, w_bf, x, gm, tm=512):
    T, K = a.shape
    D = w_bf.shape[1]
    seg_len = T // gm.shape[0]
    return pl.pallas_call(
        _out_proj_kernel,
        grid=(T // tm,),
        in_specs=[pl.BlockSpec((tm, K), lambda i: (i, 0)),
                  pl.BlockSpec((K, D), lambda i: (0, 0)),
                  pl.BlockSpec((tm, D), lambda i: (i, 0)),
                  pl.BlockSpec((1, 1, D), lambda i: ((i * tm) // seg_len, 0, 0))],
        out_specs=pl.BlockSpec((tm, D), lambda i: (i, 0)),
        out_shape=jax.ShapeDtypeStruct((T, D), F32),
        compiler_params=pltpu.CompilerParams(dimension_semantics=("parallel",),
                                             vmem_limit_bytes=40 * 1024 * 1024),
        name="out_proj",
    )(a, w_bf, x, gm)


def _rmsnorm(x, g):
    xf = x.astype(F32)
    y = xf * lax.rsqrt(jnp.mean(xf * xf, axis=-1, keepdims=True) + EPS)
    return (y * g.astype(F32)).astype(x.dtype)


def _rope_tables(L):
    rows = L // GRID_W
    row = jnp.broadcast_to(jnp.arange(rows, dtype=F32)[:, None], (rows, GRID_W)).reshape(L)
    col = jnp.broadcast_to(jnp.arange(GRID_W, dtype=F32)[None, :], (rows, GRID_W)).reshape(L)
    n_axis = HEAD_DIM // 4
    inv = ROPE_THETA ** (-jnp.arange(n_axis, dtype=F32) / n_axis)
    ang = jnp.concatenate([row[:, None] * inv, col[:, None] * inv], axis=-1)
    return jnp.cos(ang), jnp.sin(ang)


def _apply_rope(x, cos, sin):
    B, L, H, _ = x.shape
    xp = x.astype(F32).reshape(B, L, H, HEAD_DIM // 2, 2)
    a, b = xp[..., 0], xp[..., 1]
    c = cos[None, :, None, :]
    s = sin[None, :, None, :]
    out = jnp.stack([a * c - b * s, a * s + b * c], axis=-1).reshape(B, L, H, HEAD_DIM)
    return out.astype(x.dtype)


GROUP = N_HEADS // N_KV_HEADS
_NT = (((1,), (1,)), ((), ()))


def _attn_kernel(q_ref, k_ref, v_ref, o_ref):
    k = k_ref[0, 0]
    v = v_ref[0, 0]
    outs = []
    for g in range(GROUP):
        qg = q_ref[0, :, g * HEAD_DIM:(g + 1) * HEAD_DIM]
        s = lax.dot_general(qg, k, _NT, preferred_element_type=F32) * (HEAD_DIM ** -0.5)
        p = jnp.exp(s - jnp.max(s, axis=-1, keepdims=True))
        l = jnp.sum(p, axis=-1, keepdims=True)
        outs.append(jnp.dot(p.astype(BF16), v, preferred_element_type=F32) / l)
    o_ref[0] = jnp.concatenate(outs, axis=-1)


def _attention(q, k, v, tq=256):
    B, Lq, H, Dh = q.shape
    Lk = k.shape[1]
    qf = q.reshape(B, Lq, H * Dh).astype(BF16)
    kt = k.transpose(0, 2, 1, 3).astype(BF16)
    vt = v.transpose(0, 2, 1, 3).astype(BF16)
    return pl.pallas_call(
        _attn_kernel,
        grid=(B, N_KV_HEADS, Lq // tq),
        in_specs=[pl.BlockSpec((1, tq, GROUP * Dh), lambda b, kh, i: (b, i, kh)),
                  pl.BlockSpec((1, 1, Lk, Dh), lambda b, kh, i: (b, kh, 0, 0)),
                  pl.BlockSpec((1, 1, Lk, Dh), lambda b, kh, i: (b, kh, 0, 0))],
        out_specs=pl.BlockSpec((1, tq, GROUP * Dh), lambda b, kh, i: (b, i, kh)),
        out_shape=jax.ShapeDtypeStruct((B, Lq, H * Dh), F32),
        compiler_params=pltpu.CompilerParams(dimension_semantics=("parallel", "parallel", "parallel"),
                                             vmem_limit_bytes=48 * 1024 * 1024),
        name="attention",
    )(qf, kt, vt)


def _short_conv(z, w, b):
    zp = jnp.pad(z, ((0, 0), (1, 1), (0, 0)))
    return zp[:, :-2] * w[0] + zp[:, 1:-1] * w[1] + zp[:, 2:] * w[2] + b


def _hyena_filter_fft(L, fw1, fb1, fr1, fw2, fb2, fr2, fw3, decay):
    t = jnp.arange(L, dtype=F32)
    tn = t / (L - 1)
    w = 2.0 * jnp.pi * t / L
    f = jnp.linspace(1e-4, HY_BANDS - 1, HY_BANDS, dtype=F32)
    zf = w[:, None] * f[None, :]
    feats = jnp.concatenate([tn[:, None], jnp.cos(zf), -jnp.sin(zf)], axis=-1)
    h = jnp.sin(fr1 * (feats @ fw1 + fb1))
    h = jnp.sin(fr2 * (h @ fw2 + fb2))
    h = (h @ fw3).reshape(L, 2, HY_CH)
    h = h * jnp.exp(-tn[:, None, None] * jnp.abs(decay))
    fwd, bwd = h[:, 0], h[:, 1]
    full = jnp.concatenate([fwd, jnp.zeros((1, HY_CH), F32), bwd[:0:-1]], axis=0)
    return jnp.fft.rfft(full, axis=0)


def _hyena(z, P):
    B, L, _ = z.shape
    z = _short_conv(z, P['hy_conv_w'], P['hy_conv_b'])
    x0, x1, v = jnp.split(z, 3, axis=-1)
    u = (x1 * v).astype(F32)
    Hf = _hyena_filter_fft(L, P['hy_fw1'], P['hy_fb1'], P['hy_freq1'], P['hy_fw2'],
                           P['hy_fb2'], P['hy_freq2'], P['hy_fw3'], P['hy_decay'])
    U = jnp.fft.rfft(u, n=2 * L, axis=1)
    conv = jnp.fft.irfft(U * Hf[None], n=2 * L, axis=1)[:, :L]
    return x0 * (conv + u * P['hy_skip'])


def _gmlp_kernel(z_ref, g_ref, ws_ref, bs_ref, o_ref):
    for n in range(z_ref.shape[0] // CHUNK):
        rows = slice(n * CHUNK, (n + 1) * CHUNK)
        z = jax.nn.gelu(z_ref[rows, :])
        u = z[:, :SG_W]
        v = z[:, SG_W:]
        v = (v * lax.rsqrt(jnp.mean(v * v, axis=-1, keepdims=True) + EPS) * g_ref[...]).astype(BF16)
        outs = []
        for h in range(SG_GROUPS):
            vh = v[:, h * SG_GROUP_W:(h + 1) * SG_GROUP_W]
            outs.append(jnp.dot(ws_ref[h], vh, preferred_element_type=F32) + bs_ref[h])
        o_ref[rows, :] = u * jnp.concatenate(outs, axis=-1)


def _chunk_gmlp(proj, g, ws, bs, tm=512):
    T = proj.shape[0]
    col_block = HY_END // (2 * SG_W)
    return pl.pallas_call(
        _gmlp_kernel,
        grid=(T // tm,),
        in_specs=[pl.BlockSpec((tm, 2 * SG_W), lambda i: (i, col_block)),
                  pl.BlockSpec((1, SG_W), lambda i: (0, 0)),
                  pl.BlockSpec((SG_GROUPS, CHUNK, CHUNK), lambda i: (0, 0, 0)),
                  pl.BlockSpec((SG_GROUPS, CHUNK, 1), lambda i: (0, 0, 0))],
        out_specs=pl.BlockSpec((tm, SG_W), lambda i: (i, 0)),
        out_shape=jax.ShapeDtypeStruct((T, SG_W), F32),
        compiler_params=pltpu.CompilerParams(dimension_semantics=("parallel",)),
        name="chunk_gmlp",
    )(proj, g.reshape(1, SG_W), ws.astype(BF16), bs[:, :, None])


N_EXPERTS = N_KEYS * N_KEYS
NEG_INF = float('-inf')
_PAIRS = [(i, j) for i in range(PEER_TOPK) for j in range(PEER_TOPK) if (i + 1) * (j + 1) <= PEER_TOPK]


def _router_kernel(x_ref, sh_ref, sc_ref, g_ref, wqt_ref, keys_ref,
                   hb_ref, cnt1_ref, e1n_ref, rk2_ref, e2_ref,
                   q_scr, s_scr, r_scr, a_scr, c_scr, z_scr):
    tR = x_ref.shape[0]
    x = x_ref[...]
    y = x * lax.rsqrt(jnp.mean(x * x, axis=-1, keepdims=True) + EPS) * g_ref[...]
    hb = (y * (1.0 + sc_ref[0]) + sh_ref[0]).astype(BF16)
    hb_ref[...] = hb
    q_scr[...] = lax.dot_general(wqt_ref[...], hb, _NT, preferred_element_type=F32).astype(BF16)
    iota_n = lax.broadcasted_iota(jnp.int32, (N_KEYS, tR), 0).astype(F32)

    iota_r = lax.broadcasted_iota(jnp.int32, (PEER_TOPK, tR), 0).astype(F32)

    def topk_round(exact_ties, r, carry):
        s, rank, tops = carry
        rf = lax.convert_element_type(r, F32)
        m = jnp.max(s, axis=0, keepdims=True)
        sel = s == m
        if exact_ties:
            sel = iota_n == jnp.min(jnp.where(sel, iota_n, float(N_KEYS)), axis=0, keepdims=True)
        return jnp.where(sel, NEG_INF, s), jnp.where(sel, rf, rank), jnp.where(iota_r == rf, m, tops)

    def topk(s, exact_ties):
        init = (s, jnp.full((N_KEYS, tR), float(PEER_TOPK), F32), jnp.zeros((PEER_TOPK, tR), F32))
        _, rank, tops = lax.fori_loop(0, PEER_TOPK, functools.partial(topk_round, exact_ties), init)
        return rank, tops

    for hp in range(2 * PEER_HEADS):
        qh = q_scr[hp * PEER_HALF:(hp + 1) * PEER_HALF, :]
        s = jnp.dot(keys_ref[hp], qh, preferred_element_type=F32)
        s_scr[hp] = s
        rank, tops = topk(s, False)
        n_ranked = jnp.sum(jnp.where(rank < float(PEER_TOPK), 1.0, 0.0), axis=0, keepdims=True)
        rank, tops = lax.cond(jnp.max(n_ranked) > float(PEER_TOPK),
                              lambda: topk(s_scr[hp], True), lambda: (rank, tops))
        r_scr[hp] = rank
        base = ((hp % 2) * PEER_HEADS + hp // 2) * PEER_TOPK
        for c in range(tR // 128):
            a_scr[c, base:base + PEER_TOPK, :] = tops[:, c * 128:(c + 1) * 128]

    def a_rows(start, size, stride):
        return jnp.concatenate([a_scr[c, pl.ds(start, size, stride=stride), :] for c in range(tR // 128)], axis=1)

    a1 = [a_rows(i, PEER_HEADS, PEER_TOPK) for i in range(PEER_TOPK)]
    a2 = [a_rows(PEER_HEADS * PEER_TOPK + j, PEER_HEADS, PEER_TOPK) for j in range(PEER_TOPK)]
    vals = [a1[i] + a2[j] for (i, j) in _PAIRS]
    flat = [float(i * PEER_TOPK + j) for (i, j) in _PAIRS]
    picked = [jnp.zeros((PEER_HEADS, tR), F32) for _ in _PAIRS]
    for r in range(PEER_TOPK):
        m = functools.reduce(jnp.maximum, vals)
        idx = functools.reduce(jnp.minimum, [jnp.where(v == m, f, 1e9) for v, f in zip(vals, flat)])
        for k in range(len(_PAIRS)):
            sel = idx == flat[k]
            vals[k] = jnp.where(sel, NEG_INF, vals[k])
            picked[k] = jnp.where(sel, 1.0, picked[k])
    e1 = [jnp.exp(a1[i] - a1[0]) for i in range(PEER_TOPK)]
    e2 = [jnp.exp(a2[j] - a2[0]) for j in range(PEER_TOPK)]
    z = jnp.zeros((PEER_HEADS, tR), F32)
    for i in range(PEER_TOPK):
        row = [k for k, (pi, _) in enumerate(_PAIRS) if pi == i]
        c_scr[i] = functools.reduce(lambda a, b: a + b, [picked[k] for k in row])
        z = z + e1[i] * functools.reduce(lambda a, b: a + b, [picked[k] * e2[_PAIRS[k][1]] for k in row])
    z_scr[...] = 1.0 / z

    for h in range(PEER_HEADS):
        rank1 = r_scr[2 * h]
        rank2 = r_scr[2 * h + 1]
        cnt = jnp.zeros((N_KEYS, tR), F32)
        for r in range(PEER_TOPK):
            cnt = jnp.where(rank1 == float(r), c_scr[r, h:h + 1, :], cnt)
        cnt1_ref[h] = cnt
        top1 = a1[0][h:h + 1]
        top2 = a2[0][h:h + 1]
        e1d = jnp.exp(s_scr[2 * h] - top1) * z_scr[h:h + 1, :]
        e1n_ref[h] = jnp.where(rank1 < float(PEER_TOPK), e1d, 0.0)
        rk2_ref[h] = rank2
        e2_ref[h] = jnp.where(rank2 < float(PEER_TOPK), jnp.exp(s_scr[2 * h + 1] - top2), 0.0)


def _router(x, sh, sc, g, wqt, keys, tR=256):
    T, D = x.shape
    seg_len = T // sh.shape[0]
    seg = lambda i: ((i * tR) // seg_len, 0, 0)
    gshape = jax.ShapeDtypeStruct((PEER_HEADS, N_KEYS, T), F32)
    gspec = pl.BlockSpec((PEER_HEADS, N_KEYS, tR), lambda i: (0, 0, i))
    return pl.pallas_call(
        _router_kernel,
        grid=(T // tR,),
        in_specs=[pl.BlockSpec((tR, D), lambda i: (i, 0)),
                  pl.BlockSpec((1, 1, D), seg),
                  pl.BlockSpec((1, 1, D), seg),
                  pl.BlockSpec((1, D), lambda i: (0, 0)),
                  pl.BlockSpec(wqt.shape, lambda i: (0, 0)),
                  pl.BlockSpec(keys.shape, lambda i: (0, 0, 0))],
        out_specs=[pl.BlockSpec((tR, D), lambda i: (i, 0)), gspec, gspec, gspec, gspec],
        out_shape=[jax.ShapeDtypeStruct((T, D), BF16), gshape, gshape, gshape, gshape],
        scratch_shapes=[pltpu.VMEM((PEER_HEADS * PEER_KEY_DIM, tR), BF16),
                        pltpu.VMEM((2 * PEER_HEADS, N_KEYS, tR), F32),
                        pltpu.VMEM((2 * PEER_HEADS, N_KEYS, tR), F32),
                        pltpu.VMEM((tR // 128, 2 * PEER_HEADS * PEER_TOPK, 128), F32),
                        pltpu.VMEM((PEER_TOPK, PEER_HEADS, tR), F32),
                        pltpu.VMEM((PEER_HEADS, tR), F32)],
        compiler_params=pltpu.CompilerParams(dimension_semantics=("parallel",)),
        name="peer_router",
    )(x, sh, sc, g, wqt, keys)


def _experts_kernel(hb_ref, u_ref, vt_ref, cnt1_ref, e1n_ref, rk2_ref, e2_ref, x_ref, gf_ref,
                    o_ref, acc_ref, w_ref):
    i = pl.program_id(1)
    eT, tT = w_ref.shape
    rows = eT // N_KEYS

    @pl.when(i == 0)
    def _():
        acc_ref[...] = jnp.zeros_like(acc_ref)

    a_t = lax.dot_general(u_ref[...], hb_ref[...], _NT, preferred_element_type=F32)
    row0 = pl.multiple_of(i * rows, rows)
    for c in range(tT // 128):
        cols = slice(c * 128, (c + 1) * 128)
        cnt = [cnt1_ref[h, pl.ds(row0, rows), cols] for h in range(PEER_HEADS)]
        e1 = [e1n_ref[h, pl.ds(row0, rows), cols] for h in range(PEER_HEADS)]
        for r in range(rows):
            g = None
            for h in range(PEER_HEADS):
                gh = jnp.where(rk2_ref[h, :, cols] < cnt[h][r:r + 1], e2_ref[h, :, cols] * e1[h][r:r + 1], 0.0)
                g = gh if g is None else g + gh
            a = a_t[r * N_KEYS:(r + 1) * N_KEYS, cols]
            w_ref[r * N_KEYS:(r + 1) * N_KEYS, cols] = (g * jax.nn.gelu(a)).astype(BF16)
    acc_ref[...] += jnp.dot(vt_ref[...], w_ref[...], preferred_element_type=F32)

    @pl.when(i == pl.num_programs(1) - 1)
    def _():
        o_ref[...] = x_ref[...] + gf_ref[0] * acc_ref[...].T


def _experts(hb, u_bf, vt_bf, cnt1, e1n, rk2, e2, x, gf, tT=512, eT=1024):
    T, D = x.shape
    seg_len = T // gf.shape[0]
    gspec = pl.BlockSpec((PEER_HEADS, N_KEYS, tT), lambda j, i: (0, 0, j))
    return pl.pallas_call(
        _experts_kernel,
        grid=(T // tT, N_EXPERTS // eT),
        in_specs=[pl.BlockSpec((tT, D), lambda j, i: (j, 0)),
                  pl.BlockSpec((eT, D), lambda j, i: (i, 0)),
                  pl.BlockSpec((D, eT), lambda j, i: (0, i)),
                  gspec, gspec, gspec, gspec,
                  pl.BlockSpec((tT, D), lambda j, i: (j, 0)),
                  pl.BlockSpec((1, 1, D), lambda j, i: ((j * tT) // seg_len, 0, 0))],
        out_specs=pl.BlockSpec((tT, D), lambda j, i: (j, 0)),
        out_shape=jax.ShapeDtypeStruct((T, D), F32),
        scratch_shapes=[pltpu.VMEM((D, tT), F32), pltpu.VMEM((eT, tT), BF16)],
        compiler_params=pltpu.CompilerParams(dimension_semantics=("parallel", "arbitrary"),
                                             vmem_limit_bytes=48 * 1024 * 1024),
        name="peer_experts",
    )(hb, u_bf, vt_bf, cnt1, e1n, rk2, e2, x, gf)


def _peer_block(x, sh_f, sc_f, g_f, P):
    B, L, D = x.shape
    xf = x.reshape(B * L, D)
    hb, cnt1, e1n, rk2, e2 = _router(xf, sh_f, sc_f, P['norm_ffn_g'].reshape(1, D), P['peer_wqt'], P['peer_keys_bf'])
    out = _experts(hb, P['peer_u_bf'], P['peer_vt_bf'], cnt1, e1n, rk2, e2, xf, g_f)
    return out.reshape(B, L, D)


def _layer(x, mod, P, rope, k_ext, v_ext):
    B, L, D = x.shape
    sh_m, sc_m, g_m, sh_f, sc_f, g_f = jnp.split(mod, N_MOD, axis=-1)
    proj = _in_proj(x.reshape(B * L, D), sh_m, sc_m, P['norm_mix_g'].reshape(1, D), P['w_in']).reshape(B, L, -1)
    q = _rmsnorm(proj[..., :Q_END].reshape(B, L, N_HEADS, HEAD_DIM), P['q_norm_g'])
    k = _rmsnorm(proj[..., Q_END:K_END].reshape(B, L, N_KV_HEADS, HEAD_DIM), P['k_norm_g'])
    v = proj[..., K_END:V_END].reshape(B, L, N_KV_HEADS, HEAD_DIM)
    if rope is None:
        attn = _attention(q, k, v)
    else:
        cos, sin = rope
        q_r = _apply_rope(q, cos, sin)
        k_r = _apply_rope(k, cos, sin)
        k_all = jnp.concatenate([k_ext, k_r], axis=1)
        v_all = jnp.concatenate([v_ext, v], axis=1)
        attn = _attention(q_r, k_all, v_all)
    hy = _hyena(proj[..., V_END:HY_END], P)
    sg = _chunk_gmlp(proj.reshape(B * L, -1), P['sg_norm_g'], P['sg_w'], P['sg_b']).reshape(B, L, SG_W)
    cat = jnp.concatenate([attn, hy, sg], axis=-1)
    x = _out_proj(cat.reshape(B * L, -1), P['w_out'], x.reshape(B * L, D), g_m).reshape(B, L, D)
    x = _peer_block(x, sh_f, sc_f, g_f, P)
    return x, k, v


def kernel(x_prompt, x_sample, cache_k, cache_v, c, c_ctx, norm_mix_g, norm_ffn_g, ada_w, ada_b,
           w_in, w_out, q_norm_g, k_norm_g, hy_conv_w, hy_conv_b, hy_fw1, hy_fb1, hy_freq1,
           hy_fw2, hy_fb2, hy_freq2, hy_fw3, hy_decay, hy_skip, sg_norm_g, sg_w, sg_b,
           peer_wq, peer_keys, peer_u, peer_v):
    rope = _rope_tables(x_sample.shape[1])
    y_p = x_prompt
    y_s = x_sample
    ks_ctx = []
    vs_ctx = []
    for l in range(DEPTH):
        P = {
            'norm_mix_g': norm_mix_g[l], 'norm_ffn_g': norm_ffn_g[l],
            'w_in': w_in[l].astype(BF16), 'w_out': w_out[l].astype(BF16),
            'q_norm_g': q_norm_g[l], 'k_norm_g': k_norm_g[l],
            'hy_conv_w': hy_conv_w[l], 'hy_conv_b': hy_conv_b[l],
            'hy_fw1': hy_fw1[l], 'hy_fb1': hy_fb1[l], 'hy_freq1': hy_freq1[l],
            'hy_fw2': hy_fw2[l], 'hy_fb2': hy_fb2[l], 'hy_freq2': hy_freq2[l],
            'hy_fw3': hy_fw3[l], 'hy_decay': hy_decay[l], 'hy_skip': hy_skip[l],
            'sg_norm_g': sg_norm_g[l], 'sg_w': sg_w[l], 'sg_b': sg_b[l],
            'peer_wqt': peer_wq[l].T.astype(BF16),
            'peer_keys_bf': peer_keys[l].reshape(2 * PEER_HEADS, N_KEYS, PEER_HALF).astype(BF16),
            'peer_u_bf': peer_u[l].astype(BF16), 'peer_vt_bf': peer_v[l].T.astype(BF16),
        }
        mod_ctx = (jax.nn.silu(c_ctx)[None, :] @ ada_w[l] + ada_b[l])[:, None, :]
        mod_lat = (jax.nn.silu(c) @ ada_w[l] + ada_b[l])[:, None, :]
        y_p, k_l, v_l = _layer(y_p, mod_ctx, P, None, None, None)
        ks_ctx.append(k_l)
        vs_ctx.append(v_l)
        y_s, _, _ = _layer(y_s, mod_lat, P, rope, cache_k[:, l], cache_v[:, l])
    new_cache_k = jnp.stack(ks_ctx, axis=1)
    new_cache_v = jnp.stack(vs_ctx, axis=1)
    return (y_p, y_s, new_cache_k, new_cache_v)
```

```python
import functools

import jax
import jax.numpy as jnp
from jax import lax
from jax.experimental import pallas as pl
from jax.experimental.pallas import tpu as pltpu

D_MODEL = 1024
DEPTH = 2
GRID_W = 64
N_MOD = 6
EPS = 1e-6
N_HEADS = 8
N_KV_HEADS = 2
HEAD_DIM = 64
ATTN_W = N_HEADS * HEAD_DIM
KV_W = N_KV_HEADS * HEAD_DIM
Q_BLOCK = 128
ROPE_THETA = 10000.0
HY_CH = 256
HY_BANDS = 16
SG_W = 256
SG_GROUPS = 4
SG_GROUP_W = SG_W // SG_GROUPS
CHUNK = 128
Q_END = ATTN_W
K_END = Q_END + KV_W
V_END = K_END + KV_W
HY_END = V_END + 3 * HY_CH
PEER_HEADS = 8
N_KEYS = 128
PEER_KEY_DIM = 128
PEER_HALF = PEER_KEY_DIM // 2
PEER_TOPK = 16
TOKEN_BLOCK = 128

F32 = jnp.float32
BF16 = jnp.bfloat16


def _in_proj_kernel(x_ref, sh_ref, sc_ref, g_ref, w_ref, o_ref):
    x = x_ref[...]
    y = x * lax.rsqrt(jnp.mean(x * x, axis=-1, keepdims=True) + EPS) * g_ref[...]
    h = (y * (1.0 + sc_ref[0]) + sh_ref[0]).astype(BF16)
    o_ref[...] = jnp.dot(h, w_ref[...], preferred_element_type=F32)


def _in_proj(x, sh, sc, g, w_bf, tm=512):
    T, D = x.shape
    N = w_bf.shape[1]
    seg_len = T // sh.shape[0]
    seg = lambda i: ((i * tm) // seg_len, 0, 0)
    return pl.pallas_call(
        _in_proj_kernel,
        grid=(T // tm,),
        in_specs=[pl.BlockSpec((tm, D), lambda i: (i, 0)),
                  pl.BlockSpec((1, 1, D), seg),
                  pl.BlockSpec((1, 1, D), seg),
                  pl.BlockSpec((1, D), lambda i: (0, 0)),
                  pl.BlockSpec((D, N), lambda i: (0, 0))],
        out_specs=pl.BlockSpec((tm, N), lambda i: (i, 0)),
        out_shape=jax.ShapeDtypeStruct((T, N), F32),
        compiler_params=pltpu.CompilerParams(dimension_semantics=("parallel",),
                                             vmem_limit_bytes=40 * 1024 * 1024),
        name="in_proj",
    )(x, sh, sc, g, w_bf)


def _out_proj_kernel(a_ref, w_ref, x_ref, gm_ref, o_ref):
    mixed = jnp.dot(a_ref[...].astype(BF16), w_ref[...], preferred_element_type=F32)
    o_ref[...] = x_ref[...] + gm_ref[0] * mixed


def _out_proj(a, w_bf, x, gm, tm=512):
    T, K = a.shape
    D = w_bf.shape[1]
    seg_len = T // gm.shape[0]
    return pl.pallas_call(
        _out_proj_kernel,
        grid=(T // tm,),
        in_specs=[pl.BlockSpec((tm, K), lambda i: (i, 0)),
                  pl.BlockSpec((K, D), lambda i: (0, 0)),
                  pl.BlockSpec((tm, D), lambda i: (i, 0)),
                  pl.BlockSpec((1, 1, D), lambda i: ((i * tm) // seg_len, 0, 0))],
        out_specs=pl.BlockSpec((tm, D), lambda i: (i, 0)),
        out_shape=jax.ShapeDtypeStruct((T, D), F32),
        compiler_params=pltpu.CompilerParams(dimension_semantics=("parallel",),
                                             vmem_limit_bytes=40 * 1024 * 1024),
        name="out_proj",
    )(a, w_bf, x, gm)


def _rmsnorm(x, g):
    xf = x.astype(F32)
    y = xf * lax.rsqrt(jnp.mean(xf * xf, axis=-1, keepdims=True) + EPS)
    return (y * g.astype(F32)).astype(x.dtype)


def _rope_tables(L):
    rows = L // GRID_W
    row = jnp.broadcast_to(jnp.arange(rows, dtype=F32)[:, None], (rows, GRID_W)).reshape(L)
    col = jnp.broadcast_to(jnp.arange(GRID_W, dtype=F32)[None, :], (rows, GRID_W)).reshape(L)
    n_axis = HEAD_DIM // 4
    inv = ROPE_THETA ** (-jnp.arange(n_axis, dtype=F32) / n_axis)
    ang = jnp.concatenate([row[:, None] * inv, col[:, None] * inv], axis=-1)
    return jnp.cos(ang), jnp.sin(ang)


def _apply_rope(x, cos, sin):
    B, L, H, _ = x.shape
    xp = x.astype(F32).reshape(B, L, H, HEAD_DIM // 2, 2)
    a, b = xp[..., 0], xp[..., 1]
    c = cos[None, :, None, :]
    s = sin[None, :, None, :]
    out = jnp.stack([a * c - b * s, a * s + b * c], axis=-1).reshape(B, L, H, HEAD_DIM)
    return out.astype(x.dtype)


GROUP = N_HEADS // N_KV_HEADS
_NT = (((1,), (1,)), ((), ()))


def _attn_kernel(q_ref, k_ref, v_ref, o_ref):
    k = k_ref[0, 0]
    v = v_ref[0, 0]
    outs = []
    for g in range(GROUP):
        qg = q_ref[0, :, g * HEAD_DIM:(g + 1) * HEAD_DIM]
        s = lax.dot_general(qg, k, _NT, preferred_element_type=F32)
        p = jnp.exp(s - jnp.max(s, axis=-1, keepdims=True))
        l = jnp.sum(p, axis=-1, keepdims=True)
        outs.append(jnp.dot(p.astype(BF16), v, preferred_element_type=F32) / l)
    o_ref[0] = jnp.concatenate(outs, axis=-1)


def _attention(q, k, v, tq=256):
    B, Lq, H, Dh = q.shape
    Lk = k.shape[1]
    qf = (q.reshape(B, Lq, H * Dh) * (HEAD_DIM ** -0.5)).astype(BF16)
    kt = k.transpose(0, 2, 1, 3).astype(BF16)
    vt = v.transpose(0, 2, 1, 3).astype(BF16)
    return pl.pallas_call(
        _attn_kernel,
        grid=(B, N_KV_HEADS, Lq // tq),
        in_specs=[pl.BlockSpec((1, tq, GROUP * Dh), lambda b, kh, i: (b, i, kh)),
                  pl.BlockSpec((1, 1, Lk, Dh), lambda b, kh, i: (b, kh, 0, 0)),
                  pl.BlockSpec((1, 1, Lk, Dh), lambda b, kh, i: (b, kh, 0, 0))],
        out_specs=pl.BlockSpec((1, tq, GROUP * Dh), lambda b, kh, i: (b, i, kh)),
        out_shape=jax.ShapeDtypeStruct((B, Lq, H * Dh), F32),
        compiler_params=pltpu.CompilerParams(dimension_semantics=("parallel", "parallel", "parallel"),
                                             vmem_limit_bytes=48 * 1024 * 1024),
        name="attention",
    )(qf, kt, vt)


def _short_conv(z, w, b):
    zp = jnp.pad(z, ((0, 0), (1, 1), (0, 0)))
    return zp[:, :-2] * w[0] + zp[:, 1:-1] * w[1] + zp[:, 2:] * w[2] + b


def _hyena_filter_fft(L, fw1, fb1, fr1, fw2, fb2, fr2, fw3, decay):
    t = jnp.arange(L, dtype=F32)
    tn = t / (L - 1)
    w = 2.0 * jnp.pi * t / L
    f = jnp.linspace(1e-4, HY_BANDS - 1, HY_BANDS, dtype=F32)
    zf = w[:, None] * f[None, :]
    feats = jnp.concatenate([tn[:, None], jnp.cos(zf), -jnp.sin(zf)], axis=-1)
    h = jnp.sin(fr1 * (feats @ fw1 + fb1))
    h = jnp.sin(fr2 * (h @ fw2 + fb2))
    h = (h @ fw3).reshape(L, 2, HY_CH)
    h = h * jnp.exp(-tn[:, None, None] * jnp.abs(decay))
    fwd, bwd = h[:, 0], h[:, 1]
    full = jnp.concatenate([fwd, jnp.zeros((1, HY_CH), F32), bwd[:0:-1]], axis=0)
    return jnp.fft.rfft(full, axis=0)


def _hyena(z, P):
    B, L, _ = z.shape
    z = _short_conv(z, P['hy_conv_w'], P['hy_conv_b'])
    x0, x1, v = jnp.split(z, 3, axis=-1)
    u = (x1 * v).astype(F32)
    Hf = _hyena_filter_fft(L, P['hy_fw1'], P['hy_fb1'], P['hy_freq1'], P['hy_fw2'],
                           P['hy_fb2'], P['hy_freq2'], P['hy_fw3'], P['hy_decay'])
    U = jnp.fft.rfft(u, n=2 * L, axis=1)
    conv = jnp.fft.irfft(U * Hf[None], n=2 * L, axis=1)[:, :L]
    return x0 * (conv + u * P['hy_skip'])


def _gmlp_kernel(z_ref, g_ref, ws_ref, bs_ref, o_ref):
    for n in range(z_ref.shape[0] // CHUNK):
        rows = slice(n * CHUNK, (n + 1) * CHUNK)
        z = jax.nn.gelu(z_ref[rows, :])
        u = z[:, :SG_W]
        v = z[:, SG_W:]
        v = (v * lax.rsqrt(jnp.mean(v * v, axis=-1, keepdims=True) + EPS) * g_ref[...]).astype(BF16)
        outs = []
        for h in range(SG_GROUPS):
            vh = v[:, h * SG_GROUP_W:(h + 1) * SG_GROUP_W]
            outs.append(jnp.dot(ws_ref[h], vh, preferred_element_type=F32) + bs_ref[h])
        o_ref[rows, :] = u * jnp.concatenate(outs, axis=-1)


def _chunk_gmlp(proj, g, ws, bs, tm=512):
    T = proj.shape[0]
    col_block = HY_END // (2 * SG_W)
    return pl.pallas_call(
        _gmlp_kernel,
        grid=(T // tm,),
        in_specs=[pl.BlockSpec((tm, 2 * SG_W), lambda i: (i, col_block)),
                  pl.BlockSpec((1, SG_W), lambda i: (0, 0)),
                  pl.BlockSpec((SG_GROUPS, CHUNK, CHUNK), lambda i: (0, 0, 0)),
                  pl.BlockSpec((SG_GROUPS, CHUNK, 1), lambda i: (0, 0, 0))],
        out_specs=pl.BlockSpec((tm, SG_W), lambda i: (i, 0)),
        out_shape=jax.ShapeDtypeStruct((T, SG_W), F32),
        compiler_params=pltpu.CompilerParams(dimension_semantics=("parallel",)),
        name="chunk_gmlp",
    )(proj, g.reshape(1, SG_W), ws.astype(BF16), bs[:, :, None])


N_EXPERTS = N_KEYS * N_KEYS
NEG_INF = float('-inf')
_PAIRS = [(i, j) for i in range(PEER_TOPK) for j in range(PEER_TOPK) if (i + 1) * (j + 1) <= PEER_TOPK]


def _router_kernel(x_ref, sh_ref, sc_ref, g_ref, wqt_ref, keys_ref,
                   hb_ref, cnt1_ref, e1n_ref, rk2_ref, e2_ref,
                   q_scr, s_scr, r_scr, a_scr, c_scr, z_scr):
    tR = x_ref.shape[0]
    x = x_ref[...]
    y = x * lax.rsqrt(jnp.mean(x * x, axis=-1, keepdims=True) + EPS) * g_ref[...]
    hb = (y * (1.0 + sc_ref[0]) + sh_ref[0]).astype(BF16)
    hb_ref[...] = hb
    q_scr[...] = lax.dot_general(wqt_ref[...], hb, _NT, preferred_element_type=F32).astype(BF16)
    iota_n = lax.broadcasted_iota(jnp.int32, (N_KEYS, tR), 0).astype(F32)

    iota_r = lax.broadcasted_iota(jnp.int32, (PEER_TOPK, tR), 0).astype(F32)

    def topk_round(exact_ties, r, carry):
        s, rank, tops = carry
        rf = lax.convert_element_type(r, F32)
        m = jnp.max(s, axis=0, keepdims=True)
        sel = s == m
        if exact_ties:
            sel = iota_n == jnp.min(jnp.where(sel, iota_n, float(N_KEYS)), axis=0, keepdims=True)
        return jnp.where(sel, NEG_INF, s), jnp.where(sel, rf, rank), jnp.where(iota_r == rf, m, tops)

    def topk(s, exact_ties):
        init = (s, jnp.full((N_KEYS, tR), float(PEER_TOPK), F32), jnp.zeros((PEER_TOPK, tR), F32))
        _, rank, tops = lax.fori_loop(0, PEER_TOPK, functools.partial(topk_round, exact_ties), init)
        return rank, tops

    for hp in range(2 * PEER_HEADS):
        qh = q_scr[hp * PEER_HALF:(hp + 1) * PEER_HALF, :]
        s = jnp.dot(keys_ref[hp], qh, preferred_element_type=F32)
        s_scr[hp] = s
        rank, tops = topk(s, False)
        n_ranked = jnp.sum(jnp.where(rank < float(PEER_TOPK), 1.0, 0.0), axis=0, keepdims=True)
        rank, tops = lax.cond(jnp.max(n_ranked) > float(PEER_TOPK),
                              lambda: topk(s_scr[hp], True), lambda: (rank, tops))
        r_scr[hp] = rank
        base = ((hp % 2) * PEER_HEADS + hp // 2) * PEER_TOPK
        for c in range(tR // 128):
            a_scr[c, base:base + PEER_TOPK, :] = tops[:, c * 128:(c + 1) * 128]

    def a_rows(start, size, stride):
        return jnp.concatenate([a_scr[c, pl.ds(start, size, stride=stride), :] for c in range(tR // 128)], axis=1)

    a1 = [a_rows(i, PEER_HEADS, PEER_TOPK) for i in range(PEER_TOPK)]
    a2 = [a_rows(PEER_HEADS * PEER_TOPK + j, PEER_HEADS, PEER_TOPK) for j in range(PEER_TOPK)]
    vals = [a1[i] + a2[j] for (i, j) in _PAIRS]
    flat = [float(i * PEER_TOPK + j) for (i, j) in _PAIRS]
    picked = [jnp.zeros((PEER_HEADS, tR), F32) for _ in _PAIRS]
    for r in range(PEER_TOPK):
        m = functools.reduce(jnp.maximum, vals)
        idx = functools.reduce(jnp.minimum, [jnp.where(v == m, f, 1e9) for v, f in zip(vals, flat)])
        for k in range(len(_PAIRS)):
            sel = idx == flat[k]
            vals[k] = jnp.where(sel, NEG_INF, vals[k])
            picked[k] = jnp.where(sel, 1.0, picked[k])
    e1 = [jnp.exp(a1[i] - a1[0]) for i in range(PEER_TOPK)]
    e2 = [jnp.exp(a2[j] - a2[0]) for j in range(PEER_TOPK)]
    z = jnp.zeros((PEER_HEADS, tR), F32)
    for i in range(PEER_TOPK):
        row = [k for k, (pi, _) in enumerate(_PAIRS) if pi == i]
        c_scr[i] = functools.reduce(lambda a, b: a + b, [picked[k] for k in row])
        z = z + e1[i] * functools.reduce(lambda a, b: a + b, [picked[k] * e2[_PAIRS[k][1]] for k in row])
    z_scr[...] = 1.0 / z

    for h in range(PEER_HEADS):
        rank1 = r_scr[2 * h]
        rank2 = r_scr[2 * h + 1]
        cnt = jnp.zeros((N_KEYS, tR), F32)
        for r in range(PEER_TOPK):
            cnt = jnp.where(rank1 == float(r), c_scr[r, h:h + 1, :], cnt)
        cnt1_ref[h] = cnt
        top1 = a1[0][h:h + 1]
        top2 = a2[0][h:h + 1]
        e1d = jnp.exp(s_scr[2 * h] - top1) * z_scr[h:h + 1, :]
        e1n_ref[h] = jnp.where(rank1 < float(PEER_TOPK), e1d, 0.0)
        rk2_ref[h] = rank2
        e2_ref[h] = jnp.where(rank2 < float(PEER_TOPK), jnp.exp(s_scr[2 * h + 1] - top2), 0.0)


def _router(x, sh, sc, g, wqt, keys, tR=256):
    T, D = x.shape
    seg_len = T // sh.shape[0]
    seg = lambda i: ((i * tR) // seg_len, 0, 0)
    gshape = jax.ShapeDtypeStruct((PEER_HEADS, N_KEYS, T), F32)
    gspec = pl.BlockSpec((PEER_HEADS, N_KEYS, tR), lambda i: (0, 0, i))
    return pl.pallas_call(
        _router_kernel,
        grid=(T // tR,),
        in_specs=[pl.BlockSpec((tR, D), lambda i: (i, 0)),
                  pl.BlockSpec((1, 1, D), seg),
                  pl.BlockSpec((1, 1, D), seg),
                  pl.BlockSpec((1, D), lambda i: (0, 0)),
                  pl.BlockSpec(wqt.shape, lambda i: (0, 0)),
                  pl.BlockSpec(keys.shape, lambda i: (0, 0, 0))],
        out_specs=[pl.BlockSpec((tR, D), lambda i: (i, 0)), gspec, gspec, gspec, gspec],
        out_shape=[jax.ShapeDtypeStruct((T, D), BF16), gshape, gshape, gshape, gshape],
        scratch_shapes=[pltpu.VMEM((PEER_HEADS * PEER_KEY_DIM, tR), BF16),
                        pltpu.VMEM((2 * PEER_HEADS, N_KEYS, tR), F32),
                        pltpu.VMEM((2 * PEER_HEADS, N_KEYS, tR), F32),
                        pltpu.VMEM((tR // 128, 2 * PEER_HEADS * PEER_TOPK, 128), F32),
                        pltpu.VMEM((PEER_TOPK, PEER_HEADS, tR), F32),
                        pltpu.VMEM((PEER_HEADS, tR), F32)],
        compiler_params=pltpu.CompilerParams(dimension_semantics=("parallel",)),
        name="peer_router",
    )(x, sh, sc, g, wqt, keys)


def _experts_kernel(hb_ref, u_ref, vt_ref, cnt1_ref, e1n_ref, rk2_ref, e2_ref, x_ref, gf_ref,
                    o_ref, acc_ref, w_ref):
    i = pl.program_id(1)
    eT, tT = w_ref.shape
    rows = eT // N_KEYS

    @pl.when(i == 0)
    def _():
        acc_ref[...] = jnp.zeros_like(acc_ref)

    a_t = lax.dot_general(u_ref[...], hb_ref[...], _NT, preferred_element_type=F32)
    row0 = pl.multiple_of(i * rows, rows)
    for c in range(tT // 128):
        cols = slice(c * 128, (c + 1) * 128)
        cnt = [cnt1_ref[h, pl.ds(row0, rows), cols] for h in range(PEER_HEADS)]
        e1 = [e1n_ref[h, pl.ds(row0, rows), cols] for h in range(PEER_HEADS)]
        for r in range(rows):
            g = None
            for h in range(PEER_HEADS):
                gh = jnp.where(rk2_ref[h, :, cols] < cnt[h][r:r + 1], e2_ref[h, :, cols] * e1[h][r:r + 1], 0.0)
                g = gh if g is None else g + gh
            a = a_t[r * N_KEYS:(r + 1) * N_KEYS, cols]
            w_ref[r * N_KEYS:(r + 1) * N_KEYS, cols] = (g * jax.nn.gelu(a)).astype(BF16)
    acc_ref[...] += jnp.dot(vt_ref[...], w_ref[...], preferred_element_type=F32)

    @pl.when(i == pl.num_programs(1) - 1)
    def _():
        o_ref[...] = x_ref[...] + gf_ref[0] * acc_ref[...].T


def _experts(hb, u_bf, vt_bf, cnt1, e1n, rk2, e2, x, gf, tT=512, eT=1024):
    T, D = x.shape
    seg_len = T // gf.shape[0]
    gspec = pl.BlockSpec((PEER_HEADS, N_KEYS, tT), lambda j, i: (0, 0, j))
    return pl.pallas_call(
        _experts_kernel,
        grid=(T // tT, N_EXPERTS // eT),
        in_specs=[pl.BlockSpec((tT, D), lambda j, i: (j, 0)),
                  pl.BlockSpec((eT, D), lambda j, i: (i, 0)),
                  pl.BlockSpec((D, eT), lambda j, i: (0, i)),
                  gspec, gspec, gspec, gspec,
                  pl.BlockSpec((tT, D), lambda j, i: (j, 0)),
                  pl.BlockSpec((1, 1, D), lambda j, i: ((j * tT) // seg_len, 0, 0))],
        out_specs=pl.BlockSpec((tT, D), lambda j, i: (j, 0)),
        out_shape=jax.ShapeDtypeStruct((T, D), F32),
        scratch_shapes=[pltpu.VMEM((D, tT), F32), pltpu.VMEM((eT, tT), BF16)],
        compiler_params=pltpu.CompilerParams(dimension_semantics=("parallel", "arbitrary"),
                                             vmem_limit_bytes=48 * 1024 * 1024),
        name="peer_experts",
    )(hb, u_bf, vt_bf, cnt1, e1n, rk2, e2, x, gf)


def _peer_block(x, sh_f, sc_f, g_f, P):
    B, L, D = x.shape
    xf = x.reshape(B * L, D)
    hb, cnt1, e1n, rk2, e2 = _router(xf, sh_f, sc_f, P['norm_ffn_g'].reshape(1, D), P['peer_wqt'], P['peer_keys_bf'])
    out = _experts(hb, P['peer_u_bf'], P['peer_vt_bf'], cnt1, e1n, rk2, e2, xf, g_f)
    return out.reshape(B, L, D)


def _layer(x, mod, P, rope, k_ext, v_ext):
    B, L, D = x.shape
    sh_m, sc_m, g_m, sh_f, sc_f, g_f = jnp.split(mod, N_MOD, axis=-1)
    proj = _in_proj(x.reshape(B * L, D), sh_m, sc_m, P['norm_mix_g'].reshape(1, D), P['w_in']).reshape(B, L, -1)
    q = _rmsnorm(proj[..., :Q_END].reshape(B, L, N_HEADS, HEAD_DIM), P['q_norm_g'])
    k = _rmsnorm(proj[..., Q_END:K_END].reshape(B, L, N_KV_HEADS, HEAD_DIM), P['k_norm_g'])
    v = proj[..., K_END:V_END].reshape(B, L, N_KV_HEADS, HEAD_DIM)
    if rope is None:
        attn = _attention(q, k, v)
    else:
        cos, sin = rope
        q_r = _apply_rope(q, cos, sin)
        k_r = _apply_rope(k, cos, sin)
        k_all = jnp.concatenate([k_ext, k_r], axis=1)
        v_all = jnp.concatenate([v_ext, v], axis=1)
        attn = _attention(q_r, k_all, v_all)
    hy = _hyena(proj[..., V_END:HY_END], P)
    sg = _chunk_gmlp(proj.reshape(B * L, -1), P['sg_norm_g'], P['sg_w'], P['sg_b']).reshape(B, L, SG_W)
    cat = jnp.concatenate([attn, hy, sg], axis=-1)
    x = _out_proj(cat.reshape(B * L, -1), P['w_out'], x.reshape(B * L, D), g_m).reshape(B, L, D)
    x = _peer_block(x, sh_f, sc_f, g_f, P)
    return x, k, v


def kernel(x_prompt, x_sample, cache_k, cache_v, c, c_ctx, norm_mix_g, norm_ffn_g, ada_w, ada_b,
           w_in, w_out, q_norm_g, k_norm_g, hy_conv_w, hy_conv_b, hy_fw1, hy_fb1, hy_freq1,
           hy_fw2, hy_fb2, hy_freq2, hy_fw3, hy_decay, hy_skip, sg_norm_g, sg_w, sg_b,
           peer_wq, peer_keys, peer_u, peer_v):
    rope = _rope_tables(x_sample.shape[1])
    y_p = x_prompt
    y_s = x_sample
    ks_ctx = []
    vs_ctx = []
    for l in range(DEPTH):
        P = {
            'norm_mix_g': norm_mix_g[l], 'norm_ffn_g': norm_ffn_g[l],
            'w_in': w_in[l].astype(BF16), 'w_out': w_out[l].astype(BF16),
            'q_norm_g': q_norm_g[l], 'k_norm_g': k_norm_g[l],
            'hy_conv_w': hy_conv_w[l], 'hy_conv_b': hy_conv_b[l],
            'hy_fw1': hy_fw1[l], 'hy_fb1': hy_fb1[l], 'hy_freq1': hy_freq1[l],
            'hy_fw2': hy_fw2[l], 'hy_fb2': hy_fb2[l], 'hy_freq2': hy_freq2[l],
            'hy_fw3': hy_fw3[l], 'hy_decay': hy_decay[l], 'hy_skip': hy_skip[l],
            'sg_norm_g': sg_norm_g[l], 'sg_w': sg_w[l], 'sg_b': sg_b[l],
            'peer_wqt': peer_wq[l].T.astype(BF16),
            'peer_keys_bf': peer_keys[l].reshape(2 * PEER_HEADS, N_KEYS, PEER_HALF).astype(BF16),
            'peer_u_bf': peer_u[l].astype(BF16), 'peer_vt_bf': peer_v[l].T.astype(BF16),
        }
        mod_ctx = (jax.nn.silu(c_ctx)[None, :] @ ada_w[l] + ada_b[l])[:, None, :]
        mod_lat = (jax.nn.silu(c) @ ada_w[l] + ada_b[l])[:, None, :]
        y_p, k_l, v_l = _layer(y_p, mod_ctx, P, None, None, None)
        ks_ctx.append(k_l)
        vs_ctx.append(v_l)
        y_s, _, _ = _layer(y_s, mod_lat, P, rope, cache_k[:, l], cache_v[:, l])
    new_cache_k = jnp.stack(ks_ctx, axis=1)
    new_cache_v = jnp.stack(vs_ctx, axis=1)
    return (y_p, y_s, new_cache_k, new_cache_v)
```
